```python
import jax
import jax.numpy as jnp
from jax import lax
import numpy as np

D_MODEL = 1024
BATCH = 32
SEQ = 2048
DEPTH = 2
DEC_BATCH = 4
DEC_SEQ = 8192
PAST_LEN = 128

F32 = jnp.float32
EPS = 1e-6
N_EVEN = (DEPTH + 1) // 2
N_ODD = DEPTH // 2

A_WIDTH = D_MODEL // 2
A_GROUPS = 4
A_GROUP_DIM = A_WIDTH // A_GROUPS
CHUNK = 128
B_WIDTH = D_MODEL // 2
CONV_W = 31
EVEN_IN = 2 * A_WIDTH + 2 * B_WIDTH
EVEN_OUT = A_WIDTH + B_WIDTH
C_HEADS = 4
C_HEAD_DIM = D_MODEL // C_HEADS
C_WIDTH = C_HEADS * C_HEAD_DIM
C_CHUNK = 128
N_GATES = 4 * C_HEADS
ODD_IN = 4 * C_WIDTH + N_GATES
FORGET_BIAS = 3.0
X_HEADS = 4
X_HEAD_DIM = D_MODEL // X_HEADS
MEM_LEN = 256
N_EXPERTS = 16
EXPERT_FF = 2048
CAPACITY_FACTOR = 2

kernel_name = "hybrid_bidir_gmlp_conformer_mlstm_ecmoe"


def rmsnorm(x, g):
    xf = x.astype(F32)
    y = xf * lax.rsqrt(jnp.mean(xf * xf, axis=-1, keepdims=True) + EPS)
    return (y * g.astype(F32)).astype(x.dtype)


def layernorm(x, g, b):
    xf = x.astype(F32)
    mu = jnp.mean(xf, axis=-1, keepdims=True)
    var = jnp.mean(jnp.square(xf - mu), axis=-1, keepdims=True)
    y = (xf - mu) * lax.rsqrt(var + EPS)
    return (y * g.astype(F32) + b.astype(F32)).astype(x.dtype)


def spatial_gating(u, v, ws, bs, ln_g, ln_b):
    bn, s, _ = v.shape
    v = layernorm(v, ln_g, ln_b)
    vc = v.reshape(bn, s // CHUNK, CHUNK, A_GROUPS, A_GROUP_DIM)
    mixed = jnp.einsum('gpq,bnqgc->bnpgc', ws, vc) + bs.T[None, None, :, :, None]
    return u * mixed.reshape(bn, s, A_WIDTH)


def even_mixer(h, w_in, ws, bs, a_ln_g, a_ln_b, conv_w, conv_b, b_ln_g, b_ln_b, w_out):
    z = h @ w_in
    a_u, a_v, b_val, b_gate = jnp.split(z, [A_WIDTH, 2 * A_WIDTH, 2 * A_WIDTH + B_WIDTH], axis=-1)
    a_out = spatial_gating(jax.nn.gelu(a_u), jax.nn.gelu(a_v), ws, bs, a_ln_g, a_ln_b)
    g = b_val * jax.nn.sigmoid(b_gate)
    c = lax.conv_general_dilated(
        g, conv_w[:, None, :].astype(g.dtype), window_strides=(1,),
        padding=[(CONV_W // 2, CONV_W // 2)],
        dimension_numbers=('NWC', 'WIO', 'NWC'),
        feature_group_count=B_WIDTH) + conv_b
    b_out = jax.nn.silu(layernorm(c, b_ln_g, b_ln_b))
    return jnp.concatenate([a_out, b_out], axis=-1) @ w_out


def mlstm_scan(q, k, v, i_pre, f_pre):
    bn, s, nh, dh = q.shape
    nc = s // C_CHUNK
    L = C_CHUNK

    def chunks4(t):
        return t.astype(F32).reshape(bn, nc, L, nh, dh).transpose(1, 0, 3, 2, 4)

    def chunks3(t):
        return t.astype(F32).reshape(bn, nc, L, nh).transpose(1, 0, 3, 2)

    tril = jnp.tril(jnp.ones((L, L), dtype=bool))

    def step(carry, xs):
        cmat, nvec, m = carry
        qc, kc, vc, ic, fc = xs
        bcum = jnp.cumsum(jax.nn.log_sigmoid(fc), axis=-1)
        dmat = bcum[..., :, None] - bcum[..., None, :] + ic[..., None, :]
        dmat = jnp.where(tril, dmat, -jnp.inf)
        inter = bcum + m[..., None]
        m_row = jnp.maximum(jnp.max(dmat, axis=-1), inter)
        smat = jnp.einsum('bhld,bhjd->bhlj', qc, kc) * jnp.exp(dmat - m_row[..., None])
        w_inter = jnp.exp(inter - m_row)
        num = jnp.einsum('bhlj,bhjd->bhld', smat, vc) + w_inter[..., None] * jnp.einsum('bhde,bhle->bhld', cmat, qc)
        den = jnp.sum(smat, axis=-1) + w_inter * jnp.einsum('bhd,bhld->bhl', nvec, qc)
        hc = num / jnp.maximum(jnp.abs(den), jnp.exp(-m_row))[..., None]
        b_last = bcum[..., -1]
        g = b_last[..., None] - bcum + ic
        m_new = jnp.maximum(b_last + m, jnp.max(g, axis=-1))
        wg = jnp.exp(g - m_new[..., None])
        decay = jnp.exp(b_last + m - m_new)
        c_new = decay[..., None, None] * cmat + jnp.einsum('bhl,bhld,bhle->bhde', wg, vc, kc)
        n_new = decay[..., None] * nvec + jnp.einsum('bhl,bhld->bhd', wg, kc)
        return (c_new, n_new, m_new), hc

    init = (jnp.zeros((bn, nh, dh, dh), F32), jnp.zeros((bn, nh, dh), F32), jnp.zeros((bn, nh), F32))
    _, hs = lax.scan(step, init, (chunks4(q), chunks4(k), chunks4(v), chunks3(i_pre), chunks3(f_pre)))
    return hs.transpose(1, 0, 3, 2, 4).reshape(bn, s, nh, dh)


def odd_mixer(h, w_in, b_gate, hnorm_g, w_out):
    bn, s, _ = h.shape
    z = h @ w_in
    q, k, v, o, gates = jnp.split(z, [C_WIDTH, 2 * C_WIDTH, 3 * C_WIDTH, 4 * C_WIDTH], axis=-1)
    shp = (bn, s, C_HEADS, C_HEAD_DIM)
    q = q.reshape(shp)
    k = k.reshape(shp) * (C_HEAD_DIM ** -0.5)
    v = v.reshape(shp)
    gates = (gates.astype(F32) + b_gate.astype(F32)).reshape(bn, s, 4, C_HEADS)
    h_fwd = mlstm_scan(q, k, v, gates[:, :, 0], gates[:, :, 1])
    rev = lambda t: jnp.flip(t, axis=1)
    h_bwd = rev(mlstm_scan(rev(q), rev(k), rev(v), rev(gates[:, :, 2]), rev(gates[:, :, 3])))
    hs = rmsnorm(h_fwd + h_bwd, hnorm_g).astype(h.dtype)
    out = jax.nn.sigmoid(o) * hs.reshape(bn, s, C_WIDTH)
    return out @ w_out


def cross_attend(h, mem, wq, wkv, wo):
    bn, s, _ = h.shape
    m_len = mem.shape[1]
    q = (h @ wq).reshape(bn, s, X_HEADS, X_HEAD_DIM)
    k, v = jnp.split(mem @ wkv, 2, axis=-1)
    k = k.reshape(bn, m_len, X_HEADS, X_HEAD_DIM)
    v = v.reshape(bn, m_len, X_HEADS, X_HEAD_DIM)
    scores = jnp.einsum('bshd,bmhd->bhsm', q, k).astype(F32) * (X_HEAD_DIM ** -0.5)
    p = jax.nn.softmax(scores, axis=-1).astype(v.dtype)
    o = jnp.einsum('bhsm,bmhd->bshd', p, v).reshape(bn, s, D_MODEL)
    return o @ wo


def expert_choice_ffn(h, w_router, w1, w3, w2):
    bn, s, d = h.shape
    t = bn * s
    xt = h.reshape(t, d)
    aff = jax.nn.softmax((xt @ w_router).astype(F32), axis=-1)
    cap = CAPACITY_FACTOR * t // N_EXPERTS
    gate, idx = lax.top_k(aff.T, cap)
    xe = xt[idx]
    hid = jax.nn.silu(jnp.einsum('ecd,edf->ecf', xe, w1)) * jnp.einsum('ecd,edf->ecf', xe, w3)
    ye = jnp.einsum('ecf,efd->ecd', hid, w2) * gate[..., None].astype(h.dtype)
    out = jnp.zeros_like(xt).at[idx.reshape(-1)].add(ye.reshape(-1, d))
    return out.reshape(bn, s, d)


def trunk(x, mem, norm_mix, norm_xattn, norm_ffn, even_w_in, a_ws, a_bs, a_ln_g, a_ln_b,
          b_conv_w, b_conv_b, b_ln_g, b_ln_b, even_w_out, odd_w_in, odd_b_gate, odd_hnorm_g,
          odd_w_out, xa_wq, xa_wkv, xa_wo, moe_router, moe_w1, moe_w3, moe_w2, norm_final):
    for l in range(DEPTH):
        j = l // 2
        hmix = rmsnorm(x, norm_mix[l])
        if l % 2 == 0:
            x = x + even_mixer(hmix, even_w_in[j], a_ws[j], a_bs[j], a_ln_g[j], a_ln_b[j],
                               b_conv_w[j], b_conv_b[j], b_ln_g[j], b_ln_b[j], even_w_out[j])
        else:
            x = x + odd_mixer(hmix, odd_w_in[j], odd_b_gate[j], odd_hnorm_g[j], odd_w_out[j])
        x = x + cross_attend(rmsnorm(x, norm_xattn[l]), mem, xa_wq[l], xa_wkv[l], xa_wo[l])
        x = x + expert_choice_ffn(rmsnorm(x, norm_ffn[l]), moe_router[l], moe_w1[l], moe_w3[l], moe_w2[l])
    return rmsnorm(x, norm_final)


def _normal(key, shape, scale):
    return jax.random.normal(key, shape, jnp.float32) * scale


def setup_inputs(seed: int = 0) -> dict:
    key = jax.random.key(seed)
    ks = jax.random.split(key, 32)
    D = D_MODEL
    gate_base = jnp.concatenate([jnp.zeros((C_HEADS,), F32), jnp.full((C_HEADS,), FORGET_BIAS, F32),
                                 jnp.zeros((C_HEADS,), F32), jnp.full((C_HEADS,), FORGET_BIAS, F32)])
    return {
        'x_prompt': _normal(ks[0], (BATCH, SEQ, D), 1.0),
        'x_sample': _normal(ks[1], (DEC_BATCH, DEC_SEQ, D), 1.0),
        'mem_prompt': _normal(ks[2], (BATCH, MEM_LEN, D), 1.0),
        'mem_sample': _normal(ks[3], (DEC_BATCH, MEM_LEN, D), 1.0),
        'norm_mix': 1.0 + _normal(ks[4], (DEPTH, D), 0.02),
        'norm_xattn': 1.0 + _normal(ks[5], (DEPTH, D), 0.02),
        'norm_ffn': 1.0 + _normal(ks[6], (DEPTH, D), 0.02),
        'even_w_in': _normal(ks[7], (N_EVEN, D, EVEN_IN), D ** -0.5),
        'a_ws': _normal(ks[8], (N_EVEN, A_GROUPS, CHUNK, CHUNK), 0.5 * CHUNK ** -0.5),
        'a_bs': 1.0 + _normal(ks[9], (N_EVEN, A_GROUPS, CHUNK), 0.1),
        'a_ln_g': 1.0 + _normal(ks[10], (N_EVEN, A_WIDTH), 0.02),
        'a_ln_b': _normal(ks[11], (N_EVEN, A_WIDTH), 0.02),
        'b_conv_w': _normal(ks[12], (N_EVEN, CONV_W, B_WIDTH), CONV_W ** -0.5),
        'b_conv_b': _normal(ks[13], (N_EVEN, B_WIDTH), 0.02),
        'b_ln_g': 1.0 + _normal(ks[14], (N_EVEN, B_WIDTH), 0.02),
        'b_ln_b': _normal(ks[15], (N_EVEN, B_WIDTH), 0.02),
        'even_w_out': _normal(ks[16], (N_EVEN, EVEN_OUT, D), EVEN_OUT ** -0.5),
        'odd_w_in': _normal(ks[17], (N_ODD, D, ODD_IN), D ** -0.5),
        'odd_b_gate': gate_base + _normal(ks[18], (N_ODD, N_GATES), 0.1),
        'odd_hnorm_g': 1.0 + _normal(ks[19], (N_ODD, C_HEADS, C_HEAD_DIM), 0.02),
        'odd_w_out': _normal(ks[20], (N_ODD, C_WIDTH, D), C_WIDTH ** -0.5),
        'xa_wq': _normal(ks[21], (DEPTH, D, D), D ** -0.5),
        'xa_wkv': _normal(ks[22], (DEPTH, D, 2 * D), D ** -0.5),
        'xa_wo': _normal(ks[23], (DEPTH, D, D), D ** -0.5),
        'moe_router': _normal(ks[24], (DEPTH, D, N_EXPERTS), D ** -0.5),
        'moe_w1': _normal(ks[25], (DEPTH, N_EXPERTS, D, EXPERT_FF), D ** -0.5),
        'moe_w3': _normal(ks[26], (DEPTH, N_EXPERTS, D, EXPERT_FF), D ** -0.5),
        'moe_w2': _normal(ks[27], (DEPTH, N_EXPERTS, EXPERT_FF, D), EXPERT_FF ** -0.5),
        'norm_final': 1.0 + _normal(ks[28], (D,), 0.02),
    }


def reference(x_prompt, x_sample, mem_prompt, mem_sample, norm_mix, norm_xattn, norm_ffn,
              even_w_in, a_ws, a_bs, a_ln_g, a_ln_b, b_conv_w, b_conv_b, b_ln_g, b_ln_b,
              even_w_out, odd_w_in, odd_b_gate, odd_hnorm_g, odd_w_out, xa_wq, xa_wkv, xa_wo,
              moe_router, moe_w1, moe_w3, moe_w2, norm_final):
    y_prompt = trunk(x_prompt, mem_prompt, norm_mix, norm_xattn, norm_ffn, even_w_in, a_ws, a_bs,
                     a_ln_g, a_ln_b, b_conv_w, b_conv_b, b_ln_g, b_ln_b, even_w_out, odd_w_in,
                     odd_b_gate, odd_hnorm_g, odd_w_out, xa_wq, xa_wkv, xa_wo, moe_router,
                     moe_w1, moe_w3, moe_w2, norm_final)
    y_sample = trunk(x_sample, mem_sample, norm_mix, norm_xattn, norm_ffn, even_w_in, a_ws, a_bs,
                     a_ln_g, a_ln_b, b_conv_w, b_conv_b, b_ln_g, b_ln_b, even_w_out, odd_w_in,
                     odd_b_gate, odd_hnorm_g, odd_w_out, xa_wq, xa_wkv, xa_wo, moe_router,
                     moe_w1, moe_w3, moe_w2, norm_final)
    return (y_prompt, y_sample)
```

```python
import functools

import jax
import jax.numpy as jnp
from jax import lax
from jax.experimental import pallas as pl
from jax.experimental.pallas import tpu as pltpu

F32 = jnp.float32
BF16 = jnp.bfloat16
I32 = jnp.int32
EPS = 1e-6
HIGHEST = lax.Precision.HIGHEST

D = 1024
ROW_TILE = 512
A_WIDTH = 512
A_GROUPS = 4
GROUP_DIM = 128
SG_CHUNK = 128
B_WIDTH = 512
CONV_W = 31
CONV_HALO = 16
CONV_ROWS = 64
HEADS = 4
HEAD_DIM = 256
LSTM_CHUNK = 128
N_GATES = 16
MEM_LEN = 256
N_EXPERTS = 16
EXPERT_FF = 2048
FF_CHUNK = 512
CAPACITY_FACTOR = 2
MOE_BLOCK = 512
SLOT_WINDOW = 128
MAX_WINDOWS = MOE_BLOCK // SLOT_WINDOW
SLOT_ALIGN = 8
FFN_TILE = 512
VMEM_LIMIT = 56 * 1024 * 1024

NT_DIMS = (((1,), (1,)), ((), ()))
TN_DIMS = (((0,), (0,)), ((), ()))


def _params(*sem):
    return pltpu.CompilerParams(dimension_semantics=sem, vmem_limit_bytes=VMEM_LIMIT)


def _rms(x, g):
    return x * lax.rsqrt(jnp.mean(x * x, axis=-1, keepdims=True) + EPS) * g


def _layernorm(x, g, b):
    mu = jnp.mean(x, axis=-1, keepdims=True)
    xc = x - mu
    var = jnp.mean(xc * xc, axis=-1, keepdims=True)
    return xc * lax.rsqrt(var + EPS) * g + b


def _sigmoid(x):
    return 1.0 / (1.0 + jnp.exp(-x))


def _gelu_tanh(x):
    return 0.5 * x * (1.0 + jnp.tanh(0.7978845608028654 * (x + 0.044715 * (x * x * x))))


def _log_sigmoid(x):
    return jnp.minimum(x, 0.0) - jnp.log(1.0 + jnp.exp(-jnp.abs(x)))


def _bdot(a, b):
    return jnp.dot(a, b, preferred_element_type=F32)


def _full(shape):
    return pl.BlockSpec(shape, lambda *_: (0,) * len(shape))


def _proj_kernel(x_ref, g_ref, w_ref, o_ref, *, norm):
    x = x_ref[...]
    h = (_rms(x, g_ref[...]) if norm else x).astype(BF16)
    for j in range(0, w_ref.shape[1], 1024):
        o_ref[:, j:j + 1024] = _bdot(h, w_ref[:, j:j + 1024]).astype(o_ref.dtype)


def _proj(x, g, w, *, norm, tm):
    t, n = x.shape[0], w.shape[1]
    return pl.pallas_call(
        functools.partial(_proj_kernel, norm=norm),
        grid=(t // tm,),
        in_specs=[pl.BlockSpec((tm, D), lambda i: (i, 0)), _full((1, D)), _full((D, n))],
        out_specs=pl.BlockSpec((tm, n), lambda i: (i, 0)),
        out_shape=jax.ShapeDtypeStruct((t, n), BF16),
        compiler_params=_params("parallel"),
        name="proj",
    )(x, g, w)


def _even_out_kernel(z_ref, zp_ref, zn_ref, x_ref, ws_ref, bs_ref, alg_ref, alb_ref, cw_ref, cb_ref,
                     blg_ref, blb_ref, wo_ref, o_ref, g_scr, cat_scr, *, ts, tiles_per_seq):
    pos = pl.program_id(0) % tiles_per_seq
    keep_prev = jnp.where(pos == 0, 0.0, 1.0)
    keep_next = jnp.where(pos == tiles_per_seq - 1, 0.0, 1.0)

    def glu(zz):
        zz = zz.astype(F32)
        return zz[:, :B_WIDTH] * _sigmoid(zz[:, B_WIDTH:])

    g_scr[CONV_HALO:CONV_HALO + ts, :] = glu(z_ref[:, 2 * A_WIDTH:])
    g_scr[0:CONV_HALO, :] = glu(zp_ref[...]) * keep_prev
    g_scr[CONV_HALO + ts:2 * CONV_HALO + ts, :] = glu(zn_ref[...]) * keep_next

    first_tap = CONV_HALO - CONV_W // 2
    for r0 in range(0, ts, CONV_ROWS):
        acc = jnp.zeros((CONV_ROWS, B_WIDTH), F32)
        for k in range(CONV_W):
            acc = acc + cw_ref[k:k + 1, :] * g_scr[r0 + first_tap + k:r0 + first_tap + k + CONV_ROWS, :]
        c = _layernorm(acc + cb_ref[...], blg_ref[...], blb_ref[...])
        cat_scr[r0:r0 + CONV_ROWS, A_WIDTH:] = (c * _sigmoid(c)).astype(BF16)

    for r0 in range(0, ts, SG_CHUNK):
        zz = z_ref[r0:r0 + SG_CHUNK, 0:2 * A_WIDTH].astype(F32)
        u = _gelu_tanh(zz[:, :A_WIDTH])
        v = _layernorm(_gelu_tanh(zz[:, A_WIDTH:]), alg_ref[...], alb_ref[...]).astype(BF16)
        for g in range(A_GROUPS):
            cs = slice(g * GROUP_DIM, (g + 1) * GROUP_DIM)
            mixed = _bdot(ws_ref[g], v[:, cs]) + bs_ref[:, g:g + 1]
            cat_scr[r0:r0 + SG_CHUNK, cs] = (u[:, cs] * mixed).astype(BF16)

    o_ref[...] = x_ref[...] + _bdot(cat_scr[...], wo_ref[...])


def _even_out(z, x, seq, ws, bs_t, alg, alb, cw, cb, blg, blb, wo):
    t = x.shape[0]
    ts = min(ROW_TILE, seq)
    tiles_per_seq = seq // ts
    hb = ts // CONV_HALO
    n_halo = t // CONV_HALO
    return pl.pallas_call(
        functools.partial(_even_out_kernel, ts=ts, tiles_per_seq=tiles_per_seq),
        grid=(t // ts,),
        in_specs=[
            pl.BlockSpec((ts, 2 * D), lambda i: (i, 0)),
            pl.BlockSpec((CONV_HALO, D), lambda i: (jnp.maximum(i * hb - 1, 0), 1)),
            pl.BlockSpec((CONV_HALO, D), lambda i: (jnp.minimum((i + 1) * hb, n_halo - 1), 1)),
            pl.BlockSpec((ts, D), lambda i: (i, 0)),
            _full((A_GROUPS, SG_CHUNK, SG_CHUNK)), _full((SG_CHUNK, A_GROUPS)),
            _full((1, A_WIDTH)), _full((1, A_WIDTH)),
            _full((CONV_W, B_WIDTH)), _full((1, B_WIDTH)), _full((1, B_WIDTH)), _full((1, B_WIDTH)),
            _full((D, D)),
        ],
        out_specs=pl.BlockSpec((ts, D), lambda i: (i, 0)),
        out_shape=jax.ShapeDtypeStruct((t, D), F32),
        scratch_shapes=[pltpu.VMEM((ts + 2 * CONV_HALO, B_WIDTH), F32), pltpu.VMEM((ts, D), BF16)],
        compiler_params=_params("parallel"),
        name="even_out",
    )(z, z, z, x, ws, bs_t, alg, alb, cw, cb, blg, blb, wo)


def _odd_in_kernel(x_ref, g_ref, w_ref, wg_ref, wgt_ref, bg_ref, bgt_ref, z_ref, gc_ref, gr_ref):
    h = _rms(x_ref[...], g_ref[...])
    hb = h.astype(BF16)
    for j in range(0, w_ref.shape[1], 1024):
        z_ref[:, j:j + 1024] = _bdot(hb, w_ref[:, j:j + 1024]).astype(BF16)
    gc_ref[...] = jnp.dot(h, wg_ref[...], precision=HIGHEST, preferred_element_type=F32) + bg_ref[...]
    gr_ref[...] = lax.dot_general(wgt_ref[...], h, NT_DIMS, precision=HIGHEST,
                                  preferred_element_type=F32) + bgt_ref[...]


def _odd_in(x, g, w, wg, wgt, bg, bgt, tm):
    t, n = x.shape[0], w.shape[1]
    return pl.pallas_call(
        _odd_in_kernel,
        grid=(t // tm,),
        in_specs=[pl.BlockSpec((tm, D), lambda i: (i, 0)), _full((1, D)), _full((D, n)),
                  _full((D, N_GATES)), _full((N_GATES, D)), _full((1, N_GATES)), _full((N_GATES, 1))],
        out_specs=[pl.BlockSpec((tm, n), lambda i: (i, 0)),
                   pl.BlockSpec((tm, N_GATES), lambda i: (i, 0)),
                   pl.BlockSpec((N_GATES, tm), lambda i: (0, i))],
        out_shape=[jax.ShapeDtypeStruct((t, n), BF16),
                   jax.ShapeDtypeStruct((t, N_GATES), F32),
                   jax.ShapeDtypeStruct((N_GATES, t), F32)],
        compiler_params=_params("parallel"),
        name="odd_in",
    )(x, g, w, wg, wgt, bg, bgt)


def _mlstm_kernel(qf_ref, kf_ref, vf_ref, qb_ref, kb_ref, vb_ref, gcf_ref, gcb_ref, grf_ref, grb_ref,
                  hf_ref, hb_ref, c_scr, n_scr, m_scr):
    @pl.when(pl.program_id(1) == 0)
    def _():
        c_scr[...] = jnp.zeros_like(c_scr)
        n_scr[...] = jnp.zeros_like(n_scr)
        m_scr[...] = jnp.zeros_like(m_scr)

    L = LSTM_CHUNK
    row = lax.broadcasted_iota(I32, (L, L), 0)
    col = lax.broadcasted_iota(I32, (L, L), 1)
    lower = row >= col
    upper = row <= col
    lower_f = jnp.where(lower, 1.0, 0.0)
    upper_f = jnp.where(upper, 1.0, 0.0)

    dirs = ((qf_ref, kf_ref, vf_ref, gcf_ref, grf_ref, hf_ref),
            (qb_ref, kb_ref, vb_ref, gcb_ref, grb_ref, hb_ref))
    for d, (q_ref, k_ref, v_ref, gc_ref, gr_ref, h_ref) in enumerate(dirs):
        gcol = gc_ref[...]
        grow = gr_ref[...]
        lf_col = _log_sigmoid(gcol)
        lf_row = _log_sigmoid(grow)
        if d == 0:
            b_cols = jnp.dot(lower_f, lf_col, precision=HIGHEST, preferred_element_type=F32)
            b_rows = jnp.dot(lf_row, upper_f, precision=HIGHEST, preferred_element_type=F32)
            mask, last = lower, L - 1
        else:
            b_cols = jnp.dot(upper_f, lf_col, precision=HIGHEST, preferred_element_type=F32)
            b_rows = jnp.dot(lf_row, lower_f, precision=HIGHEST, preferred_element_type=F32)
            mask, last = upper, 0
        i_off, f_off = 8 * d, 8 * d + HEADS
        for hh in range(HEADS):
            idx = d * HEADS + hh
            cs = slice(hh * HEAD_DIM, (hh + 1) * HEAD_DIM)
            q = q_ref[:, cs]
            k = (k_ref[:, cs].astype(F32) * (HEAD_DIM ** -0.5)).astype(BF16)
            v = v_ref[:, cs]
            b_col = b_cols[:, f_off + hh:f_off + hh + 1]
            b_row = b_rows[f_off + hh:f_off + hh + 1, :]
            i_col = gcol[:, i_off + hh:i_off + hh + 1]
            i_row = grow[i_off + hh:i_off + hh + 1, :]
            m = m_scr[idx][:, 0:1]
            ct = c_scr[idx]
            nrow = n_scr[idx]

            dmat = jnp.where(mask, b_col - b_row + i_row, -jnp.inf)
            inter = b_col + m
            m_row = jnp.maximum(jnp.max(dmat, axis=-1, keepdims=True), inter)
            qk = lax.dot_general(q, k, NT_DIMS, preferred_element_type=F32)
            smat = qk * jnp.exp(dmat - m_row)
            w_inter = jnp.exp(inter - m_row)
            num = _bdot(smat.astype(BF16), v) + w_inter * _bdot(q, ct.astype(BF16))
            den = (jnp.sum(smat, axis=-1, keepdims=True)
                   + w_inter * jnp.sum(q.astype(F32) * nrow, axis=-1, keepdims=True))
            hc = num / jnp.maximum(jnp.abs(den), jnp.exp(-m_row))
            h_ref[:, cs] = hc.astype(h_ref.dtype)

            b_last = b_col[last:last + 1, :]
            g_col = b_last - b_col + i_col
            m_new = jnp.maximum(b_last + m, jnp.max(g_col, axis=0, keepdims=True))
            wg = jnp.exp(g_col - m_new)
            decay = jnp.exp(b_last + m - m_new)
            kv = lax.dot_general(k, (wg * v.astype(F32)).astype(BF16), TN_DIMS, preferred_element_type=F32)
            c_scr[idx] = decay * ct + kv
            n_scr[idx] = decay * nrow + jnp.sum(wg * k.astype(F32), axis=0, keepdims=True)
            m_scr[idx] = jnp.broadcast_to(m_new, (1, 128))


def _mlstm(z, gc, gr, batch, seq):
    t = z.shape[0]
    nc = seq // LSTM_CHUNK
    L = LSTM_CHUNK

    def fwd(cb):
        return lambda b, c: (b * nc + c, cb)

    def bwd(cb):
        return lambda b, c: (b * nc + nc - 1 - c, cb)

    qkv = lambda f: [pl.BlockSpec((L, D), f(0)), pl.BlockSpec((L, D), f(1)), pl.BlockSpec((L, D), f(2))]
    return pl.pallas_call(
        _mlstm_kernel,
        grid=(batch, nc),
        in_specs=qkv(fwd) + qkv(bwd) + [
            pl.BlockSpec((L, N_GATES), fwd(0)), pl.BlockSpec((L, N_GATES), bwd(0)),
            pl.BlockSpec((N_GATES, L), lambda b, c: (0, b * nc + c)),
            pl.BlockSpec((N_GATES, L), lambda b, c: (0, b * nc + nc - 1 - c)),
        ],
        out_specs=[pl.BlockSpec((L, D), fwd(0)), pl.BlockSpec((L, D), bwd(0))],
        out_shape=[jax.ShapeDtypeStruct((t, D), BF16), jax.ShapeDtypeStruct((t, D), BF16)],
        scratch_shapes=[pltpu.VMEM((2 * HEADS, HEAD_DIM, HEAD_DIM), F32),
                        pltpu.VMEM((2 * HEADS, 1, HEAD_DIM), F32),
                        pltpu.VMEM((2 * HEADS, 1, 128), F32)],
        compiler_params=_params("parallel", "arbitrary"),
        name="mlstm",
    )(z, z, z, z, z, z, gc, gc, gr, gr)


def _odd_out_kernel(hf_ref, hb_ref, og_ref, x_ref, hg_ref, wo_ref, o_ref, cat_scr):
    s = hf_ref[...].astype(F32) + hb_ref[...].astype(F32)
    for hh in range(HEADS):
        cs = slice(hh * HEAD_DIM, (hh + 1) * HEAD_DIM)
        y = _rms(s[:, cs], hg_ref[:, cs])
        cat_scr[:, cs] = (_sigmoid(og_ref[:, cs].astype(F32)) * y).astype(BF16)
    o_ref[...] = x_ref[...] + _bdot(cat_scr[...], wo_ref[...])


def _odd_out(hf, hb, z, x, hg, wo, tm):
    t = x.shape[0]
    row = lambda i: (i, 0)
    return pl.pallas_call(
        _odd_out_kernel,
        grid=(t // tm,),
        in_specs=[pl.BlockSpec((tm, D), row), pl.BlockSpec((tm, D), row),
                  pl.BlockSpec((tm, D), lambda i: (i, 3)), pl.BlockSpec((tm, D), row),
                  _full((1, D)), _full((D, D))],
        out_specs=pl.BlockSpec((tm, D), row),
        out_shape=jax.ShapeDtypeStruct((t, D), F32),
        scratch_shapes=[pltpu.VMEM((tm, D), BF16)],
        compiler_params=_params("parallel"),
        name="odd_out",
    )(hf, hb, z, x, hg, wo)


def _xattn_kernel(x_ref, g_ref, kv_ref, wq_ref, wo_ref, o_ref, cat_scr):
    x = x_ref[...]
    h = _rms(x, g_ref[...]).astype(BF16)
    q = (_bdot(h, wq_ref[...]) * (HEAD_DIM ** -0.5)).astype(BF16)
    for hh in range(HEADS):
        cs = slice(hh * HEAD_DIM, (hh + 1) * HEAD_DIM)
        k = kv_ref[:, cs]
        v = kv_ref[:, D + hh * HEAD_DIM:D + (hh + 1) * HEAD_DIM]
        s = lax.dot_general(q[:, cs], k, NT_DIMS, preferred_element_type=F32)
        p = jnp.exp(s - jnp.max(s, axis=-1, keepdims=True))
        p = p / jnp.sum(p, axis=-1, keepdims=True)
        cat_scr[:, cs] = _bdot(p.astype(BF16), v).astype(BF16)
    o_ref[...] = x + _bdot(cat_scr[...], wo_ref[...])


def _xattn(x, g, kv, wq, wo, seq):
    t = x.shape[0]
    ts = min(ROW_TILE, seq)
    tiles_per_seq = seq // ts
    return pl.pallas_call(
        _xattn_kernel,
        grid=(t // ts,),
        in_specs=[pl.BlockSpec((ts, D), lambda i: (i, 0)), _full((1, D)),
                  pl.BlockSpec((MEM_LEN, 2 * D), lambda i: (i // tiles_per_seq, 0)),
                  _full((D, D)), _full((D, D))],
        out_specs=pl.BlockSpec((ts, D), lambda i: (i, 0)),
        out_shape=jax.ShapeDtypeStruct((t, D), F32),
        scratch_shapes=[pltpu.VMEM((ts, D), BF16)],
        compiler_params=_params("parallel"),
        name="xattn",
    )(x, g, kv, wq, wo)


def _router_kernel(x_ref, g_ref, wrt_ref, aff_ref):
    h = _rms(x_ref[...], g_ref[...])
    logits = lax.dot_general(wrt_ref[...], h, NT_DIMS, precision=HIGHEST, preferred_element_type=F32)
    p = jnp.exp(logits - jnp.max(logits, axis=0, keepdims=True))
    aff_ref[...] = p / jnp.sum(p, axis=0, keepdims=True)


def _router(x, g, wrt, tm):
    t = x.shape[0]
    return pl.pallas_call(
        _router_kernel,
        grid=(t // tm,),
        in_specs=[pl.BlockSpec((tm, D), lambda i: (i, 0)), _full((1, D)), _full((N_EXPERTS, D))],
        out_specs=pl.BlockSpec((N_EXPERTS, tm), lambda i: (0, i)),
        out_shape=jax.ShapeDtypeStruct((N_EXPERTS, t), F32),
        compiler_params=_params("parallel"),
        name="router",
    )(x, g, wrt)


def _select_kernel(aff_ref, rank_ref, p0_ref, n_ref, *, cap, nblk):
    a = aff_ref[0]
    bits = pltpu.bitcast(a, I32)
    capf = float(cap)

    def count(m):
        return jnp.sum(jnp.where(m, 1.0, 0.0), keepdims=True)

    def bisect(_, carry):
        lo, hi = carry
        mid = lo + ((hi - lo + 1) >> 1)
        ok = count(bits >= mid) >= capf
        return jnp.where(ok, mid, lo), jnp.where(ok, hi, mid - 1)

    lo, _ = lax.fori_loop(0, 31, bisect,
                          (jnp.zeros((1, 1), I32), jnp.full((1, 1), 0x7F800000, I32)))
    gt = bits > lo
    eq = bits == lo
    need = capf - count(gt)

    r = lax.broadcasted_iota(I32, (MOE_BLOCK, MOE_BLOCK), 0)
    c = lax.broadcasted_iota(I32, (MOE_BLOCK, MOE_BLOCK), 1)
    incl_mat = jnp.where(r <= c, 1.0, 0.0).astype(BF16)
    rb = lax.broadcasted_iota(I32, (nblk, nblk), 0)
    cb = lax.broadcasted_iota(I32, (nblk, nblk), 1)
    before = jnp.where(rb > cb, 1.0, 0.0)

    def block_offsets(tot):
        return jnp.dot(before, jnp.broadcast_to(tot, (nblk, 128)), precision=HIGHEST,
                       preferred_element_type=F32)

    incl_eq = _bdot(jnp.where(eq, 1.0, 0.0).astype(BF16), incl_mat)
    rank_eq = block_offsets(incl_eq[:, MOE_BLOCK - 1:MOE_BLOCK])[:, 0:1] + incl_eq - 1.0
    sel = jnp.logical_or(gt, jnp.logical_and(eq, rank_eq < need))

    incl = _bdot(jnp.where(sel, 1.0, 0.0).astype(BF16), incl_mat)
    n = incl[:, MOE_BLOCK - 1:MOE_BLOCK]
    n_aligned = jnp.floor((n + (SLOT_ALIGN - 1)) / SLOT_ALIGN) * SLOT_ALIGN
    rank_ref[0] = jnp.where(sel, incl - 1.0, -1.0).astype(I32)
    p0_ref[0] = block_offsets(n_aligned).astype(I32)
    n_ref[0] = jnp.broadcast_to(n, (nblk, 128)).astype(I32)


def _select(aff3, cap):
    _, nblk, _ = aff3.shape
    blk = lambda w: pl.BlockSpec((1, nblk, w), lambda e: (e, 0, 0))
    return pl.pallas_call(
        functools.partial(_select_kernel, cap=cap, nblk=nblk),
        grid=(N_EXPERTS,),
        in_specs=[blk(MOE_BLOCK)],
        out_specs=[blk(MOE_BLOCK), blk(128), blk(128)],
        out_shape=[jax.ShapeDtypeStruct((N_EXPERTS, nblk, MOE_BLOCK), I32),
                   jax.ShapeDtypeStruct((N_EXPERTS, nblk, 128), I32),
                   jax.ShapeDtypeStruct((N_EXPERTS, nblk, 128), I32)],
        compiler_params=_params("parallel"),
        name="select",
    )(aff3)


def _pack_kernel(p0_ref, n_ref, x_ref, g_ref, rank_ref, xe_in_ref, xe_ref, h_scr, stage, sem, pend, *, nblk):
    del xe_in_ref
    b = pl.program_id(0)

    @pl.when(b == 0)
    def _():
        for e in range(N_EXPERTS):
            pend[e] = 0

    def copy(e, start):
        return pltpu.make_async_copy(stage.at[e], xe_ref.at[e, pl.ds(start, SLOT_WINDOW), :], sem.at[e])

    h_scr[...] = _rms(x_ref[...], g_ref[...]).astype(BF16)
    slot = lax.broadcasted_iota(I32, (SLOT_WINDOW, MOE_BLOCK), 0)
    for e in range(N_EXPERTS):
        n_e = n_ref[e, b]
        p0_e = p0_ref[e, b]
        for w in range(MAX_WINDOWS):
            @pl.when(n_e > SLOT_WINDOW * w)
            def _(e=e, w=w, p0_e=p0_e):
                @pl.when(pend[e] == 1)
                def _():
                    copy(e, 0).wait()
                onehot = jnp.where(rank_ref[e, 0] - SLOT_WINDOW * w == slot, 1.0, 0.0).astype(BF16)
                stage[e] = _bdot(onehot, h_scr[...])
                copy(e, pl.multiple_of(p0_e + SLOT_WINDOW * w, SLOT_ALIGN)).start()
                pend[e] = 1

    @pl.when(b == nblk - 1)
    def _():
        for e in range(N_EXPERTS):
            @pl.when(pend[e] == 1)
            def _(e=e):
                copy(e, 0).wait()


def _pack(p0, n, x, g, rank4, xe_init):
    t = x.shape[0]
    nblk = t // MOE_BLOCK
    grid_spec = pltpu.PrefetchScalarGridSpec(
        num_scalar_prefetch=2,
        grid=(nblk,),
        in_specs=[pl.BlockSpec((MOE_BLOCK, D), lambda b, *_: (b, 0)),
                  pl.BlockSpec((1, D), lambda b, *_: (0, 0)),
                  pl.BlockSpec((N_EXPERTS, 1, 1, MOE_BLOCK), lambda b, *_: (0, b, 0, 0)),
                  pl.BlockSpec(memory_space=pl.ANY)],
        out_specs=pl.BlockSpec(memory_space=pl.ANY),
        scratch_shapes=[pltpu.VMEM((MOE_BLOCK, D), BF16),
                        pltpu.VMEM((N_EXPERTS, SLOT_WINDOW, D), F32),
                        pltpu.SemaphoreType.DMA((N_EXPERTS,)),
                        pltpu.SMEM((N_EXPERTS,), I32)],
    )
    return pl.pallas_call(
        functools.partial(_pack_kernel, nblk=nblk),
        grid_spec=grid_spec,
        out_shape=jax.ShapeDtypeStruct(xe_init.shape, F32),
        input_output_aliases={5: 0},
        compiler_params=_params("arbitrary"),
        name="pack",
    )(p0, n, x, g, rank4, xe_init)


def _ffn_kernel(pend_ref, xe_ref, w1_ref, w3_ref, w2_ref, ye_ref, *, tm):
    e, i = pl.program_id(0), pl.program_id(1)
    active = i * tm < pend_ref[e] + SLOT_WINDOW

    @pl.when(active)
    def _():
        xb = xe_ref[0].astype(BF16)
        acc = jnp.zeros((tm, D), F32)
        for f in range(0, EXPERT_FF, FF_CHUNK):
            h1 = _bdot(xb, w1_ref[0, :, f:f + FF_CHUNK])
            h3 = _bdot(xb, w3_ref[0, :, f:f + FF_CHUNK])
            hid = (h1 * _sigmoid(h1) * h3).astype(BF16)
            acc = acc + _bdot(hid, w2_ref[0, f:f + FF_CHUNK, :])
        ye_ref[0] = acc

    @pl.when(jnp.logical_not(active))
    def _():
        ye_ref[0] = jnp.zeros((tm, D), F32)


def _ffn(p_end, xe, w1, w3, w2):
    _, cap_x, _ = xe.shape
    tm = FFN_TILE
    grid_spec = pltpu.PrefetchScalarGridSpec(
        num_scalar_prefetch=1,
        grid=(N_EXPERTS, cap_x // tm),
        in_specs=[pl.BlockSpec((1, tm, D), lambda e, i, *_: (e, i, 0)),
                  pl.BlockSpec((1, D, EXPERT_FF), lambda e, i, *_: (e, 0, 0)),
                  pl.BlockSpec((1, D, EXPERT_FF), lambda e, i, *_: (e, 0, 0)),
                  pl.BlockSpec((1, EXPERT_FF, D), lambda e, i, *_: (e, 0, 0))],
        out_specs=pl.BlockSpec((1, tm, D), lambda e, i, *_: (e, i, 0)),
    )
    return pl.pallas_call(
        functools.partial(_ffn_kernel, tm=tm),
        grid_spec=grid_spec,
        out_shape=jax.ShapeDtypeStruct(xe.shape, F32),
        compiler_params=_params("parallel", "arbitrary"),
        name="ffn",
    )(p_end, xe, w1, w3, w2)


def _unpack_kernel(p0_ref, n_ref, x_ref, rank_ref, gate_ref, gfin_ref, ye_ref, o_ref, buf, sem, *, final_norm):
    b = pl.program_id(0)

    def copy(e, start):
        return pltpu.make_async_copy(ye_ref.at[e, pl.ds(start, SLOT_WINDOW), :], buf.at[e], sem.at[e])

    def window_start(e, w):
        return pl.multiple_of(p0_ref[e, b] + SLOT_WINDOW * w, SLOT_ALIGN)

    for e in range(N_EXPERTS):
        @pl.when(n_ref[e, b] > 0)
        def _(e=e):
            copy(e, window_start(e, 0)).start()

    o_ref[...] = x_ref[...]
    lane = lax.broadcasted_iota(I32, (MOE_BLOCK, SLOT_WINDOW), 1)
    for e in range(N_EXPERTS):
        n_e = n_ref[e, b]
        for w in range(MAX_WINDOWS):
            @pl.when(n_e > SLOT_WINDOW * w)
            def _(e=e, w=w):
                if w > 0:
                    copy(e, window_start(e, w)).start()
                copy(e, 0).wait()
                onehot = jnp.where(rank_ref[:, e:e + 1] - SLOT_WINDOW * w == lane, 1.0, 0.0).astype(BF16)
                o_ref[...] += gate_ref[:, e:e + 1] * _bdot(onehot, buf[e].astype(BF16))

    if final_norm:
        o_ref[...] = _rms(o_ref[...], gfin_ref[...])


def _unpack(p0, n, x, rank_t, gate_t, gfin, ye, final_norm):
    t = x.shape[0]
    grid_spec = pltpu.PrefetchScalarGridSpec(
        num_scalar_prefetch=2,
        grid=(t // MOE_BLOCK,),
        in_specs=[pl.BlockSpec((MOE_BLOCK, D), lambda b, *_: (b, 0)),
                  pl.BlockSpec((MOE_BLOCK, N_EXPERTS), lambda b, *_: (b, 0)),
                  pl.BlockSpec((MOE_BLOCK, N_EXPERTS), lambda b, *_: (b, 0)),
                  pl.BlockSpec((1, D), lambda b, *_: (0, 0)),
                  pl.BlockSpec(memory_space=pl.ANY)],
        out_specs=pl.BlockSpec((MOE_BLOCK, D), lambda b, *_: (b, 0)),
        scratch_shapes=[pltpu.VMEM((N_EXPERTS, SLOT_WINDOW, D), F32),
                        pltpu.SemaphoreType.DMA((N_EXPERTS,))],
    )
    return pl.pallas_call(
        functools.partial(_unpack_kernel, final_norm=final_norm),
        grid_spec=grid_spec,
        out_shape=jax.ShapeDtypeStruct((t, D), F32),
        compiler_params=_params("arbitrary"),
        name="unpack",
    )(p0, n, x, rank_t, gate_t, gfin, ye)


def _moe(x, g, wrt, w1, w3, w2, gfin, final_norm):
    t = x.shape[0]
    nblk = t // MOE_BLOCK
    cap = CAPACITY_FACTOR * t // N_EXPERTS
    cap_padded = cap + SLOT_ALIGN * nblk + SLOT_WINDOW
    cap_x = -(-cap_padded // FFN_TILE) * FFN_TILE

    aff_t = _router(x, g, wrt, min(ROW_TILE, t))
    rank, p0w, nw = _select(aff_t.reshape(N_EXPERTS, nblk, MOE_BLOCK), cap)
    p0, n = p0w[:, :, 0], nw[:, :, 0]
    p_end = p0[:, -1] + (n[:, -1] + SLOT_ALIGN - 1) // SLOT_ALIGN * SLOT_ALIGN

    xe = _pack(p0, n, x, g, rank.reshape(N_EXPERTS, nblk, 1, MOE_BLOCK),
               jnp.zeros((N_EXPERTS, cap_x, D), F32))
    ye = _ffn(p_end, xe, w1, w3, w2)
    return _unpack(p0, n, x, rank.reshape(N_EXPERTS, t).T, aff_t.T, gfin, ye, final_norm)


def _trunk(x3, mem3, w):
    batch, seq, _ = x3.shape
    t = batch * seq
    tm = min(ROW_TILE, seq)
    x = x3.reshape(t, D)
    mem = mem3.reshape(batch * MEM_LEN, D)
    row = lambda v: v.reshape(1, -1)
    depth = w["norm_mix"].shape[0]
    for l in range(depth):
        j = l // 2
        g_mix = row(w["norm_mix"][l])
        if l % 2 == 0:
            z = _proj(x, g_mix, w["even_w_in"][j], norm=True, tm=tm)
            x = _even_out(z, x, seq, w["a_ws"][j], w["a_bs"][j].T, row(w["a_ln_g"][j]), row(w["a_ln_b"][j]),
                          w["b_conv_w"][j], row(w["b_conv_b"][j]), row(w["b_ln_g"][j]), row(w["b_ln_b"][j]),
                          w["even_w_out"][j])
        else:
            z, gc, gr = _odd_in(x, g_mix, w["odd_w_qkvo"][j], w["odd_w_gate"][j], w["odd_w_gate"][j].T,
                                row(w["odd_b_gate"][j]), w["odd_b_gate"][j].reshape(-1, 1), tm)
            hf, hb = _mlstm(z, gc, gr, batch, seq)
            x = _odd_out(hf, hb, z, x, row(w["odd_hnorm_g"][j]), w["odd_w_out"][j], tm)
        kv = _proj(mem, g_mix, w["xa_wkv"][l], norm=False, tm=MEM_LEN)
        x = _xattn(x, row(w["norm_xattn"][l]), kv, w["xa_wq"][l], w["xa_wo"][l], seq)
        x = _moe(x, row(w["norm_ffn"][l]), w["moe_router"][l].T, w["moe_w1"][l], w["moe_w3"][l],
                 w["moe_w2"][l], row(w["norm_final"]), final_norm=(l == depth - 1))
    return x.reshape(batch, seq, D)


def kernel(x_prompt, x_sample, mem_prompt, mem_sample, norm_mix, norm_xattn, norm_ffn, even_w_in, a_ws, a_bs, a_ln_g, a_ln_b, b_conv_w, b_conv_b, b_ln_g, b_ln_b, even_w_out, odd_w_in, odd_b_gate, odd_hnorm_g, odd_w_out, xa_wq, xa_wkv, xa_wo, moe_router, moe_w1, moe_w3, moe_w2, norm_final):
    bf = lambda v: v.astype(BF16)
    n_qkvo = 4 * HEADS * HEAD_DIM
    w = dict(
        norm_mix=norm_mix, norm_xattn=norm_xattn, norm_ffn=norm_ffn, norm_final=norm_final,
        even_w_in=bf(even_w_in), a_ws=bf(a_ws), a_bs=a_bs, a_ln_g=a_ln_g, a_ln_b=a_ln_b,
        b_conv_w=b_conv_w, b_conv_b=b_conv_b, b_ln_g=b_ln_g, b_ln_b=b_ln_b, even_w_out=bf(even_w_out),
        odd_w_qkvo=bf(odd_w_in[:, :, :n_qkvo]), odd_w_gate=odd_w_in[:, :, n_qkvo:], odd_b_gate=odd_b_gate,
        odd_hnorm_g=odd_hnorm_g, odd_w_out=bf(odd_w_out),
        xa_wq=bf(xa_wq), xa_wkv=bf(xa_wkv), xa_wo=bf(xa_wo),
        moe_router=moe_router, moe_w1=bf(moe_w1), moe_w3=bf(moe_w3), moe_w2=bf(moe_w2),
    )
    return _trunk(x_prompt, mem_prompt, w), _trunk(x_sample, mem_sample, w)
```

```python
import functools

import jax
import jax.numpy as jnp
from jax import lax
from jax.experimental import pallas as pl
from jax.experimental.pallas import tpu as pltpu

F32 = jnp.float32
BF16 = jnp.bfloat16
I32 = jnp.int32
EPS = 1e-6
HIGHEST = lax.Precision.HIGHEST

D = 1024
ROW_TILE = 512
A_WIDTH = 512
A_GROUPS = 4
GROUP_DIM = 128
SG_CHUNK = 128
B_WIDTH = 512
CONV_W = 31
CONV_HALO = 16
CONV_ROWS = 64
HEADS = 4
HEAD_DIM = 256
LSTM_CHUNK = 128
N_GATES = 16
MEM_LEN = 256
N_EXPERTS = 16
EXPERT_FF = 2048
FF_CHUNK = 512
CAPACITY_FACTOR = 2
MOE_BLOCK = 512
SLOT_WINDOW = 128
MAX_WINDOWS = MOE_BLOCK // SLOT_WINDOW
SLOT_ALIGN = 8
FFN_TILE = 512
VMEM_LIMIT = 56 * 1024 * 1024

NT_DIMS = (((1,), (1,)), ((), ()))
TN_DIMS = (((0,), (0,)), ((), ()))


def _params(*sem):
    return pltpu.CompilerParams(dimension_semantics=sem, vmem_limit_bytes=VMEM_LIMIT)


def _rms(x, g):
    return x * lax.rsqrt(jnp.mean(x * x, axis=-1, keepdims=True) + EPS) * g


def _layernorm(x, g, b):
    mu = jnp.mean(x, axis=-1, keepdims=True)
    xc = x - mu
    var = jnp.mean(xc * xc, axis=-1, keepdims=True)
    return xc * lax.rsqrt(var + EPS) * g + b


def _sigmoid(x):
    return 1.0 / (1.0 + jnp.exp(-x))


def _gelu_tanh(x):
    return 0.5 * x * (1.0 + jnp.tanh(0.7978845608028654 * (x + 0.044715 * (x * x * x))))


def _log_sigmoid(x):
    return jnp.minimum(x, 0.0) - jnp.log(1.0 + jnp.exp(-jnp.abs(x)))


def _bdot(a, b):
    return jnp.dot(a, b, preferred_element_type=F32)


def _full(shape):
    return pl.BlockSpec(shape, lambda *_: (0,) * len(shape))


def _proj_kernel(x_ref, g_ref, w_ref, o_ref, *, norm):
    x = x_ref[...]
    h = (_rms(x, g_ref[...]) if norm else x).astype(BF16)
    for j in range(0, w_ref.shape[1], 1024):
        o_ref[:, j:j + 1024] = _bdot(h, w_ref[:, j:j + 1024]).astype(o_ref.dtype)


def _proj(x, g, w, *, norm, tm):
    t, n = x.shape[0], w.shape[1]
    return pl.pallas_call(
        functools.partial(_proj_kernel, norm=norm),
        grid=(t // tm,),
        in_specs=[pl.BlockSpec((tm, D), lambda i: (i, 0)), _full((1, D)), _full((D, n))],
        out_specs=pl.BlockSpec((tm, n), lambda i: (i, 0)),
        out_shape=jax.ShapeDtypeStruct((t, n), BF16),
        compiler_params=_params("parallel"),
        name="proj",
    )(x, g, w)


def _even_out_kernel(z_ref, zp_ref, zn_ref, x_ref, ws_ref, bs_ref, alg_ref, alb_ref, cw_ref, cb_ref,
                     blg_ref, blb_ref, wo_ref, o_ref, g_scr, gs_scr, cat_scr, *, ts, tiles_per_seq):
    pos = pl.program_id(0) % tiles_per_seq
    keep_prev = jnp.where(pos == 0, 0.0, 1.0)
    keep_next = jnp.where(pos == tiles_per_seq - 1, 0.0, 1.0)

    def glu(zz):
        zz = zz.astype(F32)
        return zz[:, :B_WIDTH] * _sigmoid(zz[:, B_WIDTH:])

    g_scr[CONV_HALO:CONV_HALO + ts, :] = glu(z_ref[:, 2 * A_WIDTH:])
    g_scr[0:CONV_HALO, :] = glu(zp_ref[...]) * keep_prev
    g_scr[CONV_HALO + ts:2 * CONV_HALO + ts, :] = glu(zn_ref[...]) * keep_next

    span = ts + 2 * CONV_HALO - 8
    for s in range(1, 8):
        gs_scr[s - 1] = g_scr[s:s + span, :]

    first_tap = CONV_HALO - CONV_W // 2
    for r0 in range(0, ts, CONV_ROWS):
        acc = jnp.zeros((CONV_ROWS, B_WIDTH), F32)
        for k in range(CONV_W):
            a, s = divmod(first_tap + k, 8)
            src = g_scr if s == 0 else gs_scr.at[s - 1]
            acc = acc + cw_ref[k:k + 1, :] * src[r0 + 8 * a:r0 + 8 * a + CONV_ROWS, :]
        c = _layernorm(acc + cb_ref[...], blg_ref[...], blb_ref[...])
        cat_scr[r0:r0 + CONV_ROWS, A_WIDTH:] = (c * _sigmoid(c)).astype(BF16)

    for r0 in range(0, ts, SG_CHUNK):
        zz = z_ref[r0:r0 + SG_CHUNK, 0:2 * A_WIDTH].astype(F32)
        u = _gelu_tanh(zz[:, :A_WIDTH])
        v = _layernorm(_gelu_tanh(zz[:, A_WIDTH:]), alg_ref[...], alb_ref[...]).astype(BF16)
        for g in range(A_GROUPS):
            cs = slice(g * GROUP_DIM, (g + 1) * GROUP_DIM)
            mixed = _bdot(ws_ref[g], v[:, cs]) + bs_ref[:, g:g + 1]
            cat_scr[r0:r0 + SG_CHUNK, cs] = (u[:, cs] * mixed).astype(BF16)

    o_ref[...] = x_ref[...] + _bdot(cat_scr[...], wo_ref[...])


def _even_out(z, x, seq, ws, bs_t, alg, alb, cw, cb, blg, blb, wo):
    t = x.shape[0]
    ts = min(ROW_TILE, seq)
    tiles_per_seq = seq // ts
    hb = ts // CONV_HALO
    n_halo = t // CONV_HALO
    return pl.pallas_call(
        functools.partial(_even_out_kernel, ts=ts, tiles_per_seq=tiles_per_seq),
        grid=(t // ts,),
        in_specs=[
            pl.BlockSpec((ts, 2 * D), lambda i: (i, 0)),
            pl.BlockSpec((CONV_HALO, D), lambda i: (jnp.maximum(i * hb - 1, 0), 1)),
            pl.BlockSpec((CONV_HALO, D), lambda i: (jnp.minimum((i + 1) * hb, n_halo - 1), 1)),
            pl.BlockSpec((ts, D), lambda i: (i, 0)),
            _full((A_GROUPS, SG_CHUNK, SG_CHUNK)), _full((SG_CHUNK, A_GROUPS)),
            _full((1, A_WIDTH)), _full((1, A_WIDTH)),
            _full((CONV_W, B_WIDTH)), _full((1, B_WIDTH)), _full((1, B_WIDTH)), _full((1, B_WIDTH)),
            _full((D, D)),
        ],
        out_specs=pl.BlockSpec((ts, D), lambda i: (i, 0)),
        out_shape=jax.ShapeDtypeStruct((t, D), F32),
        scratch_shapes=[pltpu.VMEM((ts + 2 * CONV_HALO, B_WIDTH), F32),
                        pltpu.VMEM((7, ts + 2 * CONV_HALO - 8, B_WIDTH), F32),
                        pltpu.VMEM((ts, D), BF16)],
        compiler_params=_params("parallel"),
        name="even_out",
    )(z, z, z, x, ws, bs_t, alg, alb, cw, cb, blg, blb, wo)


def _odd_in_kernel(x_ref, g_ref, wk_ref, wt_ref, wg_ref, wgt_ref, bg_ref, bgt_ref, k_ref, zt_ref, gc_ref, gr_ref):
    h = _rms(x_ref[...], g_ref[...])
    hb = h.astype(BF16)
    k_ref[...] = _bdot(hb, wk_ref[...]).astype(BF16)
    for j in range(0, wt_ref.shape[0], 1024):
        zt_ref[j:j + 1024, :] = lax.dot_general(wt_ref[j:j + 1024, :], hb, NT_DIMS,
                                                preferred_element_type=F32).astype(BF16)
    gc_ref[...] = jnp.dot(h, wg_ref[...], precision=HIGHEST, preferred_element_type=F32) + bg_ref[...]
    gr_ref[...] = lax.dot_general(wgt_ref[...], h, NT_DIMS, precision=HIGHEST,
                                  preferred_element_type=F32) + bgt_ref[...]


def _odd_in(x, g, wk, wt, wg, wgt, bg, bgt, tm):
    t, n = x.shape[0], wt.shape[0]
    return pl.pallas_call(
        _odd_in_kernel,
        grid=(t // tm,),
        in_specs=[pl.BlockSpec((tm, D), lambda i: (i, 0)), _full((1, D)), _full((D, D)), _full((n, D)),
                  _full((D, N_GATES)), _full((N_GATES, D)), _full((1, N_GATES)), _full((N_GATES, 1))],
        out_specs=[pl.BlockSpec((tm, D), lambda i: (i, 0)),
                   pl.BlockSpec((n, tm), lambda i: (0, i)),
                   pl.BlockSpec((tm, N_GATES), lambda i: (i, 0)),
                   pl.BlockSpec((N_GATES, tm), lambda i: (0, i))],
        out_shape=[jax.ShapeDtypeStruct((t, D), BF16),
                   jax.ShapeDtypeStruct((n, t), BF16),
                   jax.ShapeDtypeStruct((t, N_GATES), F32),
                   jax.ShapeDtypeStruct((N_GATES, t), F32)],
        compiler_params=_params("parallel"),
        name="odd_in",
    )(x, g, wk, wt, wg, wgt, bg, bgt)


def _mlstm_kernel(qf_ref, kf_ref, vf_ref, qb_ref, kb_ref, vb_ref, gcf_ref, gcb_ref, grf_ref, grb_ref,
                  hf_ref, hb_ref, c_scr, n_scr, m_scr):
    @pl.when(pl.program_id(1) == 0)
    def _():
        c_scr[...] = jnp.zeros_like(c_scr)
        n_scr[...] = jnp.zeros_like(n_scr)
        m_scr[...] = jnp.zeros_like(m_scr)

    L = LSTM_CHUNK
    row = lax.broadcasted_iota(I32, (L, L), 0)
    col = lax.broadcasted_iota(I32, (L, L), 1)
    lower = row >= col
    upper = row <= col
    lower_f = jnp.where(lower, 1.0, 0.0)
    upper_f = jnp.where(upper, 1.0, 0.0)

    dirs = ((qf_ref, kf_ref, vf_ref, gcf_ref, grf_ref, hf_ref),
            (qb_ref, kb_ref, vb_ref, gcb_ref, grb_ref, hb_ref))
    for d, (q_ref, k_ref, v_ref, gc_ref, gr_ref, h_ref) in enumerate(dirs):
        gcol = gc_ref[...]
        grow = gr_ref[...]
        lf_col = _log_sigmoid(gcol)
        lf_row = _log_sigmoid(grow)
        if d == 0:
            b_cols = jnp.dot(lower_f, lf_col, precision=HIGHEST, preferred_element_type=F32)
            b_rows = jnp.dot(lf_row, upper_f, precision=HIGHEST, preferred_element_type=F32)
            mask_t, last = upper, L - 1
        else:
            b_cols = jnp.dot(upper_f, lf_col, precision=HIGHEST, preferred_element_type=F32)
            b_rows = jnp.dot(lf_row, lower_f, precision=HIGHEST, preferred_element_type=F32)
            mask_t, last = lower, 0
        i_off, f_off = 8 * d, 8 * d + HEADS
        k_scale = HEAD_DIM ** -0.5
        for hh in range(HEADS):
            idx = d * HEADS + hh
            cs = slice(hh * HEAD_DIM, (hh + 1) * HEAD_DIM)
            k = k_ref[:, cs]
            qt = q_ref[cs, :]
            vt = v_ref[cs, :]
            b_row = b_rows[f_off + hh:f_off + hh + 1, :]
            i_row = grow[i_off + hh:i_off + hh + 1, :]
            src_col = gcol[:, i_off + hh:i_off + hh + 1] - b_cols[:, f_off + hh:f_off + hh + 1]
            m = m_scr[idx][:, 0:1]
            cmat = c_scr[idx]
            nrows = n_scr[idx]

            dmat_t = jnp.where(mask_t, b_row + src_col, -jnp.inf)
            inter = b_row + m
            m_row = jnp.maximum(jnp.max(dmat_t, axis=0, keepdims=True), inter)
            smat_t = _bdot(k, qt) * (jnp.exp(dmat_t - m_row) * k_scale)
            w_inter = jnp.exp(inter - m_row)
            num_t = _bdot(vt, smat_t.astype(BF16)) + w_inter * _bdot(cmat.astype(BF16), qt)
            den = (jnp.sum(smat_t, axis=0, keepdims=True)
                   + w_inter * _bdot(nrows.astype(BF16), qt)[0:1, :])
            rden = 1.0 / jnp.maximum(jnp.abs(den), jnp.exp(-m_row))
            h_ref[cs, :] = (num_t * rden).astype(h_ref.dtype)

            b_last = b_row[:, last:last + 1]
            g_row = b_last - b_row + i_row
            m_new = jnp.maximum(b_last + m, jnp.max(g_row, axis=1, keepdims=True))
            wg = jnp.exp(g_row - m_new) * k_scale
            decay = jnp.exp(b_last + m - m_new)
            c_scr[idx] = decay * cmat + _bdot((vt.astype(F32) * wg).astype(BF16), k)
            n_scr[idx] = decay * nrows + _bdot(jnp.broadcast_to(wg, (8, L)).astype(BF16), k)
            m_scr[idx] = jnp.broadcast_to(m_new, (1, 128))


def _mlstm(k, zt, gc, gr, batch, seq):
    t = k.shape[0]
    nc = seq // LSTM_CHUNK
    L = LSTM_CHUNK
    fwd = lambda b, c: (b * nc + c, 0)
    bwd = lambda b, c: (b * nc + nc - 1 - c, 0)

    def fwd_t(rb):
        return lambda b, c: (rb, b * nc + c)

    def bwd_t(rb):
        return lambda b, c: (rb, b * nc + nc - 1 - c)

    qkv = lambda f, ft: [pl.BlockSpec((D, L), ft(0)), pl.BlockSpec((L, D), f), pl.BlockSpec((D, L), ft(1))]
    return pl.pallas_call(
        _mlstm_kernel,
        grid=(batch, nc),
        in_specs=qkv(fwd, fwd_t) + qkv(bwd, bwd_t) + [
            pl.BlockSpec((L, N_GATES), fwd), pl.BlockSpec((L, N_GATES), bwd),
            pl.BlockSpec((N_GATES, L), fwd_t(0)), pl.BlockSpec((N_GATES, L), bwd_t(0)),
        ],
        out_specs=[pl.BlockSpec((D, L), fwd_t(0)), pl.BlockSpec((D, L), bwd_t(0))],
        out_shape=[jax.ShapeDtypeStruct((D, t), BF16), jax.ShapeDtypeStruct((D, t), BF16)],
        scratch_shapes=[pltpu.VMEM((2 * HEADS, HEAD_DIM, HEAD_DIM), F32),
                        pltpu.VMEM((2 * HEADS, 8, HEAD_DIM), F32),
                        pltpu.VMEM((2 * HEADS, 1, 128), F32)],
        compiler_params=_params("parallel", "arbitrary"),
        name="mlstm",
    )(zt, k, zt, zt, k, zt, gc, gc, gr, gr)


def _odd_out_kernel(hf_ref, hb_ref, og_ref, x_ref, hg_ref, wo_ref, o_ref, cat_scr):
    tm = x_ref.shape[0]
    for hh in range(HEADS):
        rs = slice(hh * HEAD_DIM, (hh + 1) * HEAD_DIM)
        s = hf_ref[rs, :].astype(F32) + hb_ref[rs, :].astype(F32)
        inv = lax.rsqrt(jnp.mean(s * s, axis=0, keepdims=True) + EPS)
        gate = _sigmoid(og_ref[rs, :].astype(F32))
        for c0 in range(0, tm, 128):
            cs = slice(c0, c0 + 128)
            cat_scr[rs, cs] = (gate[:, cs] * (s[:, cs] * inv[:, cs]) * hg_ref[rs, :]).astype(BF16)
    o_ref[...] = x_ref[...] + lax.dot_general(cat_scr[...], wo_ref[...], TN_DIMS, preferred_element_type=F32)


def _odd_out(hf, hb, zt, x, hg_rep, wo, tm):
    t = x.shape[0]
    col = lambda i: (0, i)
    return pl.pallas_call(
        _odd_out_kernel,
        grid=(t // tm,),
        in_specs=[pl.BlockSpec((D, tm), col), pl.BlockSpec((D, tm), col),
                  pl.BlockSpec((D, tm), lambda i: (2, i)), pl.BlockSpec((tm, D), lambda i: (i, 0)),
                  _full((D, 128)), _full((D, D))],
        out_specs=pl.BlockSpec((tm, D), lambda i: (i, 0)),
        out_shape=jax.ShapeDtypeStruct((t, D), F32),
        scratch_shapes=[pltpu.VMEM((D, tm), BF16)],
        compiler_params=_params("parallel"),
        name="odd_out",
    )(hf, hb, zt, x, hg_rep, wo)


def _xattn_kernel(x_ref, g_ref, kv_ref, wq_ref, wo_ref, gr_ref, wr_ref, o_ref, aff_ref, cat_scr):
    x = x_ref[...]
    h = _rms(x, g_ref[...]).astype(BF16)
    q = (_bdot(h, wq_ref[...]) * (HEAD_DIM ** -0.5)).astype(BF16)
    for hh in range(HEADS):
        cs = slice(hh * HEAD_DIM, (hh + 1) * HEAD_DIM)
        k = kv_ref[:, cs]
        v = kv_ref[:, D + hh * HEAD_DIM:D + (hh + 1) * HEAD_DIM]
        s = lax.dot_general(q[:, cs], k, NT_DIMS, preferred_element_type=F32)
        p = jnp.exp(s - jnp.max(s, axis=-1, keepdims=True))
        p = p / jnp.sum(p, axis=-1, keepdims=True)
        cat_scr[:, cs] = _bdot(p.astype(BF16), v).astype(BF16)
    y = x + _bdot(cat_scr[...], wo_ref[...])
    o_ref[...] = y

    hr = _rms(y, gr_ref[...])
    hr_hi = hr.astype(BF16)
    hr_lo = (hr - hr_hi.astype(F32)).astype(BF16)
    main = lax.dot_general(wr_ref[...], hr_hi, NT_DIMS, preferred_element_type=F32)
    corr = lax.dot_general(wr_ref[0:N_EXPERTS, :], hr_lo, NT_DIMS, preferred_element_type=F32)
    logits = main[0:N_EXPERTS] + main[N_EXPERTS:] + corr
    p = jnp.exp(logits - jnp.max(logits, axis=0, keepdims=True))
    aff_ref[...] = p / jnp.sum(p, axis=0, keepdims=True)


def _xattn(x, g, kv, wq, wo, g_ffn, wr, seq):
    t = x.shape[0]
    ts = min(ROW_TILE, seq)
    tiles_per_seq = seq // ts
    return pl.pallas_call(
        _xattn_kernel,
        grid=(t // ts,),
        in_specs=[pl.BlockSpec((ts, D), lambda i: (i, 0)), _full((1, D)),
                  pl.BlockSpec((MEM_LEN, 2 * D), lambda i: (i // tiles_per_seq, 0)),
                  _full((D, D)), _full((D, D)), _full((1, D)), _full((2 * N_EXPERTS, D))],
        out_specs=[pl.BlockSpec((ts, D), lambda i: (i, 0)), pl.BlockSpec((N_EXPERTS, ts), lambda i: (0, i))],
        out_shape=[jax.ShapeDtypeStruct((t, D), F32), jax.ShapeDtypeStruct((N_EXPERTS, t), F32)],
        scratch_shapes=[pltpu.VMEM((ts, D), BF16)],
        compiler_params=_params("parallel"),
        name="xattn",
    )(x, g, kv, wq, wo, g_ffn, wr)


def _select_kernel(aff_ref, rank_ref, p0_ref, n_ref, *, cap, nblk):
    a = aff_ref[0]
    bits = pltpu.bitcast(a, I32)
    capf = float(cap)

    def count(m):
        return jnp.sum(jnp.where(m, 1.0, 0.0), keepdims=True)

    def bisect(_, carry):
        lo, hi = carry
        mid = lo + ((hi - lo + 1) >> 1)
        ok = count(bits >= mid) >= capf
        return jnp.where(ok, mid, lo), jnp.where(ok, hi, mid - 1)

    lo, _ = lax.fori_loop(0, 31, bisect,
                          (jnp.zeros((1, 1), I32), jnp.full((1, 1), 0x7F800000, I32)))
    gt = bits > lo
    eq = bits == lo
    need = capf - count(gt)

    r = lax.broadcasted_iota(I32, (MOE_BLOCK, MOE_BLOCK), 0)
    c = lax.broadcasted_iota(I32, (MOE_BLOCK, MOE_BLOCK), 1)
    incl_mat = jnp.where(r <= c, 1.0, 0.0).astype(BF16)
    rb = lax.broadcasted_iota(I32, (nblk, nblk), 0)
    cb = lax.broadcasted_iota(I32, (nblk, nblk), 1)
    before = jnp.where(rb > cb, 1.0, 0.0)

    def block_offsets(tot):
        return jnp.dot(before, jnp.broadcast_to(tot, (nblk, 128)), precision=HIGHEST,
                       preferred_element_type=F32)

    incl_eq = _bdot(jnp.where(eq, 1.0, 0.0).astype(BF16), incl_mat)
    rank_eq = block_offsets(incl_eq[:, MOE_BLOCK - 1:MOE_BLOCK])[:, 0:1] + incl_eq - 1.0
    sel = jnp.logical_or(gt, jnp.logical_and(eq, rank_eq < need))

    incl = _bdot(jnp.where(sel, 1.0, 0.0).astype(BF16), incl_mat)
    n = incl[:, MOE_BLOCK - 1:MOE_BLOCK]
    n_aligned = jnp.floor((n + (SLOT_ALIGN - 1)) / SLOT_ALIGN) * SLOT_ALIGN
    rank_ref[0] = jnp.where(sel, incl - 1.0, -1.0).astype(I32)
    p0_ref[0] = block_offsets(n_aligned).astype(I32)
    n_ref[0] = jnp.broadcast_to(n, (nblk, 128)).astype(I32)


def _select(aff3, cap):
    _, nblk, _ = aff3.shape
    blk = lambda w: pl.BlockSpec((1, nblk, w), lambda e: (e, 0, 0))
    return pl.pallas_call(
        functools.partial(_select_kernel, cap=cap, nblk=nblk),
        grid=(N_EXPERTS,),
        in_specs=[blk(MOE_BLOCK)],
        out_specs=[blk(MOE_BLOCK), blk(128), blk(128)],
        out_shape=[jax.ShapeDtypeStruct((N_EXPERTS, nblk, MOE_BLOCK), I32),
                   jax.ShapeDtypeStruct((N_EXPERTS, nblk, 128), I32),
                   jax.ShapeDtypeStruct((N_EXPERTS, nblk, 128), I32)],
        compiler_params=_params("parallel"),
        name="select",
    )(aff3)


def _pack_kernel(p0_ref, n_ref, pend_ref, x_ref, g_ref, rank_ref, xe_ref, h_scr, p_scr, stage, extra,
                 sem, esem, *, nblk, cap, cap_x):
    b = pl.program_id(0)
    slot = b % 2

    def copy(e, start, slot):
        return pltpu.make_async_copy(stage.at[slot, pl.ds(e * SLOT_WINDOW, SLOT_WINDOW), :],
                                     xe_ref.at[e, pl.ds(start, SLOT_WINDOW), :], sem.at[e])

    def extra_copy(e, start):
        return pltpu.make_async_copy(extra, xe_ref.at[e, pl.ds(start, SLOT_WINDOW), :], esem.at[0])

    def window_start(e, w):
        return pl.multiple_of(p0_ref[e, b] + SLOT_WINDOW * w, SLOT_ALIGN)

    h_scr[...] = _rms(x_ref[...], g_ref[...]).astype(BF16)
    slot_id = lax.broadcasted_iota(I32, (SLOT_WINDOW, MOE_BLOCK), 0)
    for e in range(N_EXPERTS):
        p_scr[e * SLOT_WINDOW:(e + 1) * SLOT_WINDOW, :] = jnp.where(
            rank_ref[e, 0] == slot_id, 1.0, 0.0).astype(BF16)
    for r0 in range(0, N_EXPERTS * SLOT_WINDOW, MOE_BLOCK):
        stage[slot, r0:r0 + MOE_BLOCK, :] = _bdot(p_scr[r0:r0 + MOE_BLOCK, :], h_scr[...])

    for e in range(N_EXPERTS):
        @pl.when(b > 0)
        def _(e=e):
            copy(e, 0, 1 - slot).wait()
        copy(e, window_start(e, 0), slot).start()
        for w in range(1, MAX_WINDOWS):
            @pl.when(n_ref[e, b] > SLOT_WINDOW * w)
            def _(e=e, w=w):
                onehot = jnp.where(rank_ref[e, 0] - SLOT_WINDOW * w == slot_id, 1.0, 0.0).astype(BF16)
                extra[...] = _bdot(onehot, h_scr[...])
                c = extra_copy(e, window_start(e, w))
                c.start()
                c.wait()

    @pl.when(b == nblk - 1)
    def _():
        for e in range(N_EXPERTS):
            copy(e, 0, slot).wait()
        extra[...] = jnp.zeros_like(extra)
        n_zero = -(-(cap_x - cap) // SLOT_WINDOW)
        last_start = cap_x - SLOT_WINDOW
        for phase in ("start", "wait"):
            for e in range(N_EXPERTS):
                for z in range(n_zero):
                    start = pend_ref[e] + SLOT_WINDOW * z

                    @pl.when(start < last_start)
                    def _(e=e, start=start, phase=phase):
                        c = extra_copy(e, pl.multiple_of(start, SLOT_ALIGN))
                        c.start() if phase == "start" else c.wait()
        for e in range(N_EXPERTS):
            extra_copy(e, last_start).start()
        for e in range(N_EXPERTS):
            extra_copy(e, last_start).wait()


def _pack(p0, n, p_end, x, g, rank4, cap_x):
    t = x.shape[0]
    nblk = t // MOE_BLOCK
    grid_spec = pltpu.PrefetchScalarGridSpec(
        num_scalar_prefetch=3,
        grid=(nblk,),
        in_specs=[pl.BlockSpec((MOE_BLOCK, D), lambda b, *_: (b, 0)),
                  pl.BlockSpec((1, D), lambda b, *_: (0, 0)),
                  pl.BlockSpec((N_EXPERTS, 1, 1, MOE_BLOCK), lambda b, *_: (0, b, 0, 0))],
        out_specs=pl.BlockSpec(memory_space=pl.ANY),
        scratch_shapes=[pltpu.VMEM((MOE_BLOCK, D), BF16),
                        pltpu.VMEM((N_EXPERTS * SLOT_WINDOW, MOE_BLOCK), BF16),
                        pltpu.VMEM((2, N_EXPERTS * SLOT_WINDOW, D), F32),
                        pltpu.VMEM((SLOT_WINDOW, D), F32),
                        pltpu.SemaphoreType.DMA((N_EXPERTS,)),
                        pltpu.SemaphoreType.DMA((1,))],
    )
    return pl.pallas_call(
        functools.partial(_pack_kernel, nblk=nblk, cap=CAPACITY_FACTOR * t // N_EXPERTS, cap_x=cap_x),
        grid_spec=grid_spec,
        out_shape=jax.ShapeDtypeStruct((N_EXPERTS, cap_x, D), F32),
        compiler_params=_params("arbitrary"),
        name="pack",
    )(p0, n, p_end, x, g, rank4)


def _ffn_kernel(pend_ref, xe_ref, w1_ref, w3_ref, w2_ref, ye_ref, *, tm):
    e, i = pl.program_id(0), pl.program_id(1)
    active = i * tm < pend_ref[e]

    @pl.when(active)
    def _():
        xb = xe_ref[0].astype(BF16)
        acc = jnp.zeros((tm, D), F32)
        for f in range(0, EXPERT_FF, FF_CHUNK):
            h1 = _bdot(xb, w1_ref[0, :, f:f + FF_CHUNK])
            h3 = _bdot(xb, w3_ref[0, :, f:f + FF_CHUNK])
            hid = (h1 * _sigmoid(h1) * h3).astype(BF16)
            acc = acc + _bdot(hid, w2_ref[0, f:f + FF_CHUNK, :])
        ye_ref[0] = acc

    @pl.when(jnp.logical_not(active))
    def _():
        ye_ref[0] = jnp.zeros((tm, D), F32)


def _ffn(p_end, xe, w1, w3, w2):
    _, cap_x, _ = xe.shape
    tm = FFN_TILE
    grid_spec = pltpu.PrefetchScalarGridSpec(
        num_scalar_prefetch=1,
        grid=(N_EXPERTS, cap_x // tm),
        in_specs=[pl.BlockSpec((1, tm, D), lambda e, i, *_: (e, i, 0)),
                  pl.BlockSpec((1, D, EXPERT_FF), lambda e, i, *_: (e, 0, 0)),
                  pl.BlockSpec((1, D, EXPERT_FF), lambda e, i, *_: (e, 0, 0)),
                  pl.BlockSpec((1, EXPERT_FF, D), lambda e, i, *_: (e, 0, 0))],
        out_specs=pl.BlockSpec((1, tm, D), lambda e, i, *_: (e, i, 0)),
    )
    return pl.pallas_call(
        functools.partial(_ffn_kernel, tm=tm),
        grid_spec=grid_spec,
        out_shape=jax.ShapeDtypeStruct(xe.shape, F32),
        compiler_params=_params("parallel", "arbitrary"),
        name="ffn",
    )(p_end, xe, w1, w3, w2)


def _unpack_kernel(p0_ref, n_ref, x_ref, rank_ref, gate_ref, gfin_ref, ye_ref, o_ref, slab, slab_bf, u_scr,
                   extra, sem, esem, *, nblk, final_norm):
    b = pl.program_id(0)
    slot = b % 2
    rows = N_EXPERTS * SLOT_WINDOW

    def copy(e, blk, slot):
        start = pl.multiple_of(p0_ref[e, blk], SLOT_ALIGN)
        return pltpu.make_async_copy(ye_ref.at[e, pl.ds(start, SLOT_WINDOW), :],
                                     slab.at[slot, pl.ds(e * SLOT_WINDOW, SLOT_WINDOW), :], sem.at[slot])

    @pl.when(b == 0)
    def _():
        for e in range(N_EXPERTS):
            copy(e, 0, 0).start()

    @pl.when(b + 1 < nblk)
    def _():
        for e in range(N_EXPERTS):
            copy(e, b + 1, 1 - slot).start()

    lane = lax.broadcasted_iota(I32, (MOE_BLOCK, SLOT_WINDOW), 1)
    for e in range(N_EXPERTS):
        u_scr[:, e * SLOT_WINDOW:(e + 1) * SLOT_WINDOW] = jnp.where(
            rank_ref[:, e:e + 1] == lane, gate_ref[:, e:e + 1], 0.0).astype(BF16)
    for e in range(N_EXPERTS):
        copy(e, b, slot).wait()
    for r0 in range(0, rows, MOE_BLOCK):
        slab_bf[r0:r0 + MOE_BLOCK, :] = slab[slot, r0:r0 + MOE_BLOCK, :].astype(BF16)
    o_ref[...] = x_ref[...] + _bdot(u_scr[...], slab_bf[...])

    for e in range(N_EXPERTS):
        for w in range(1, MAX_WINDOWS):
            @pl.when(n_ref[e, b] > SLOT_WINDOW * w)
            def _(e=e, w=w):
                start = pl.multiple_of(p0_ref[e, b] + SLOT_WINDOW * w, SLOT_ALIGN)
                c = pltpu.make_async_copy(ye_ref.at[e, pl.ds(start, SLOT_WINDOW), :], extra, esem.at[0])
                c.start()
                c.wait()
                onehot = jnp.where(rank_ref[:, e:e + 1] - SLOT_WINDOW * w == lane, 1.0, 0.0).astype(BF16)
                o_ref[...] += gate_ref[:, e:e + 1] * _bdot(onehot, extra[...].astype(BF16))

    if final_norm:
        o_ref[...] = _rms(o_ref[...], gfin_ref[...])


def _unpack(p0, n, x, rank_t, gate_t, gfin, ye, final_norm):
    t = x.shape[0]
    nblk = t // MOE_BLOCK
    rows = N_EXPERTS * SLOT_WINDOW
    grid_spec = pltpu.PrefetchScalarGridSpec(
        num_scalar_prefetch=2,
        grid=(nblk,),
        in_specs=[pl.BlockSpec((MOE_BLOCK, D), lambda b, *_: (b, 0)),
                  pl.BlockSpec((MOE_BLOCK, N_EXPERTS), lambda b, *_: (b, 0)),
                  pl.BlockSpec((MOE_BLOCK, N_EXPERTS), lambda b, *_: (b, 0)),
                  pl.BlockSpec((1, D), lambda b, *_: (0, 0)),
                  pl.BlockSpec(memory_space=pl.ANY)],
        out_specs=pl.BlockSpec((MOE_BLOCK, D), lambda b, *_: (b, 0)),
        scratch_shapes=[pltpu.VMEM((2, rows, D), F32),
                        pltpu.VMEM((rows, D), BF16),
                        pltpu.VMEM((MOE_BLOCK, rows), BF16),
                        pltpu.VMEM((SLOT_WINDOW, D), F32),
                        pltpu.SemaphoreType.DMA((2,)),
                        pltpu.SemaphoreType.DMA((1,))],
    )
    return pl.pallas_call(
        functools.partial(_unpack_kernel, nblk=nblk, final_norm=final_norm),
        grid_spec=grid_spec,
        out_shape=jax.ShapeDtypeStruct((t, D), F32),
        compiler_params=_params("arbitrary"),
        name="unpack",
    )(p0, n, x, rank_t, gate_t, gfin, ye)


def _moe(x, aff_t, g, w1, w3, w2, gfin, final_norm):
    t = x.shape[0]
    nblk = t // MOE_BLOCK
    cap = CAPACITY_FACTOR * t // N_EXPERTS
    cap_x = -(-(cap + SLOT_ALIGN * nblk + FFN_TILE) // FFN_TILE) * FFN_TILE

    rank, p0w, nw = _select(aff_t.reshape(N_EXPERTS, nblk, MOE_BLOCK), cap)
    p0, n = p0w[:, :, 0], nw[:, :, 0]
    p_end = p0[:, -1] + (n[:, -1] + SLOT_ALIGN - 1) // SLOT_ALIGN * SLOT_ALIGN

    xe = _pack(p0, n, p_end, x, g, rank.reshape(N_EXPERTS, nblk, 1, MOE_BLOCK), cap_x)
    ye = _ffn(p_end, xe, w1, w3, w2)
    return _unpack(p0, n, x, rank.reshape(N_EXPERTS, t).T, aff_t.T, gfin, ye, final_norm)


def _trunk(x3, mem3, w):
    batch, seq, _ = x3.shape
    t = batch * seq
    tm = min(ROW_TILE, seq)
    x = x3.reshape(t, D)
    mem = mem3.reshape(batch * MEM_LEN, D)
    row = lambda v: v.reshape(1, -1)
    depth = w["norm_mix"].shape[0]
    for l in range(depth):
        j = l // 2
        g_mix = row(w["norm_mix"][l])
        if l % 2 == 0:
            z = _proj(x, g_mix, w["even_w_in"][j], norm=True, tm=tm)
            x = _even_out(z, x, seq, w["a_ws"][j], w["a_bs"][j].T, row(w["a_ln_g"][j]), row(w["a_ln_b"][j]),
                          w["b_conv_w"][j], row(w["b_conv_b"][j]), row(w["b_ln_g"][j]), row(w["b_ln_b"][j]),
                          w["even_w_out"][j])
        else:
            k, zt, gc, gr = _odd_in(x, g_mix, w["odd_w_k"][j], w["odd_w_qvo_t"][j], w["odd_w_gate"][j],
                                    w["odd_w_gate"][j].T, row(w["odd_b_gate"][j]),
                                    w["odd_b_gate"][j].reshape(-1, 1), tm)
            hf, hb = _mlstm(k, zt, gc, gr, batch, seq)
            hg_rep = jnp.broadcast_to(w["odd_hnorm_g"][j].reshape(D, 1), (D, 128))
            x = _odd_out(hf, hb, zt, x, hg_rep, w["odd_w_out"][j], tm)
        kv = _proj(mem, g_mix, w["xa_wkv"][l], norm=False, tm=MEM_LEN)
        g_ffn = row(w["norm_ffn"][l])
        x, aff_t = _xattn(x, row(w["norm_xattn"][l]), kv, w["xa_wq"][l], w["xa_wo"][l], g_ffn,
                          w["moe_router_t"][l], seq)
        x = _moe(x, aff_t, g_ffn, w["moe_w1"][l], w["moe_w3"][l], w["moe_w2"][l], row(w["norm_final"]),
                 final_norm=(l == depth - 1))
    return x.reshape(batch, seq, D)


def _prepare_weights(norm_mix, norm_xattn, norm_ffn, even_w_in, a_ws, a_bs, a_ln_g, a_ln_b, b_conv_w, b_conv_b,
                     b_ln_g, b_ln_b, even_w_out, odd_w_in, odd_b_gate, odd_hnorm_g, odd_w_out, xa_wq, xa_wkv,
                     xa_wo, moe_router, moe_w1, moe_w3, moe_w2, norm_final):
    bf = lambda v: v.astype(BF16)
    router_t = jnp.swapaxes(moe_router, 1, 2)
    router_hi = bf(router_t)
    router_lo = bf(router_t - router_hi.astype(F32))
    w = dict(
        norm_mix=norm_mix, norm_xattn=norm_xattn, norm_ffn=norm_ffn, norm_final=norm_final,
        even_w_in=bf(even_w_in), a_ws=bf(a_ws), a_bs=a_bs, a_ln_g=a_ln_g, a_ln_b=a_ln_b,
        b_conv_w=b_conv_w, b_conv_b=b_conv_b, b_ln_g=b_ln_g, b_ln_b=b_ln_b, even_w_out=bf(even_w_out),
        odd_w_qvo_t=bf(jnp.swapaxes(jnp.concatenate([odd_w_in[:, :, 0:D], odd_w_in[:, :, 2 * D:4 * D]], axis=-1),
                                    1, 2)),
        odd_w_k=bf(odd_w_in[:, :, D:2 * D]),
        odd_w_gate=odd_w_in[:, :, 4 * D:], odd_b_gate=odd_b_gate,
        odd_hnorm_g=odd_hnorm_g, odd_w_out=bf(odd_w_out),
        xa_wq=bf(xa_wq), xa_wkv=bf(xa_wkv), xa_wo=bf(xa_wo),
        moe_router_t=jnp.concatenate([router_hi, router_lo], axis=1),
        moe_w1=bf(moe_w1), moe_w3=bf(moe_w3), moe_w2=bf(moe_w2),
    )
    return w


def kernel(x_prompt, x_sample, mem_prompt, mem_sample, norm_mix, norm_xattn, norm_ffn, even_w_in, a_ws, a_bs, a_ln_g, a_ln_b, b_conv_w, b_conv_b, b_ln_g, b_ln_b, even_w_out, odd_w_in, odd_b_gate, odd_hnorm_g, odd_w_out, xa_wq, xa_wkv, xa_wo, moe_router, moe_w1, moe_w3, moe_w2, norm_final):
    w = _prepare_weights(norm_mix, norm_xattn, norm_ffn, even_w_in, a_ws, a_bs, a_ln_g, a_ln_b, b_conv_w,
                         b_conv_b, b_ln_g, b_ln_b, even_w_out, odd_w_in, odd_b_gate, odd_hnorm_g, odd_w_out,
                         xa_wq, xa_wkv, xa_wo, moe_router, moe_w1, moe_w3, moe_w2, norm_final)
    return _trunk(x_prompt, mem_prompt, w), _trunk(x_sample, mem_sample, w)
```

```python
import functools

import jax
import jax.numpy as jnp
from jax import lax
from jax.experimental import pallas as pl
from jax.experimental.pallas import tpu as pltpu

F32 = jnp.float32
BF16 = jnp.bfloat16
I32 = jnp.int32
EPS = 1e-6
HIGHEST = lax.Precision.HIGHEST

D = 1024
ROW_TILE = 512
A_WIDTH = 512
A_GROUPS = 4
GROUP_DIM = 128
SG_CHUNK = 128
B_WIDTH = 512
CONV_W = 31
CONV_HALO = 16
CONV_ROWS = 64
HEADS = 4
HEAD_DIM = 256
LSTM_CHUNK = 128
N_GATES = 16
MEM_LEN = 256
N_EXPERTS = 16
EXPERT_FF = 2048
FF_CHUNK = 512
CAPACITY_FACTOR = 2
MOE_BLOCK = 1024
SLOT_WINDOW = 256
MAX_WINDOWS = MOE_BLOCK // SLOT_WINDOW
SLOT_ALIGN = 16
FFN_TILE = 512
VMEM_LIMIT = 56 * 1024 * 1024

NT_DIMS = (((1,), (1,)), ((), ()))
TN_DIMS = (((0,), (0,)), ((), ()))


def _params(*sem):
    return pltpu.CompilerParams(dimension_semantics=sem, vmem_limit_bytes=VMEM_LIMIT)


def _rms(x, g):
    return x * lax.rsqrt(jnp.mean(x * x, axis=-1, keepdims=True) + EPS) * g


def _layernorm(x, g, b):
    mu = jnp.mean(x, axis=-1, keepdims=True)
    xc = x - mu
    var = jnp.mean(xc * xc, axis=-1, keepdims=True)
    return xc * lax.rsqrt(var + EPS) * g + b


def _sigmoid(x):
    return 1.0 / (1.0 + jnp.exp(-x))


def _gelu_tanh(x):
    return 0.5 * x * (1.0 + jnp.tanh(0.7978845608028654 * (x + 0.044715 * (x * x * x))))


def _log_sigmoid(x):
    return jnp.minimum(x, 0.0) - jnp.log(1.0 + jnp.exp(-jnp.abs(x)))


def _bdot(a, b):
    return jnp.dot(a, b, preferred_element_type=F32)


def _full(shape):
    return pl.BlockSpec(shape, lambda *_: (0,) * len(shape))


def _proj_kernel(x_ref, g_ref, w_ref, o_ref, *, norm):
    x = x_ref[...]
    h = (_rms(x, g_ref[...]) if norm else x).astype(BF16)
    for j in range(0, w_ref.shape[1], 1024):
        o_ref[:, j:j + 1024] = _bdot(h, w_ref[:, j:j + 1024]).astype(o_ref.dtype)


def _proj(x, g, w, *, norm, tm):
    t, n = x.shape[0], w.shape[1]
    return pl.pallas_call(
        functools.partial(_proj_kernel, norm=norm),
        grid=(t // tm,),
        in_specs=[pl.BlockSpec((tm, D), lambda i: (i, 0)), _full((1, D)), _full((D, n))],
        out_specs=pl.BlockSpec((tm, n), lambda i: (i, 0)),
        out_shape=jax.ShapeDtypeStruct((t, n), BF16),
        compiler_params=_params("parallel"),
        name="proj",
    )(x, g, w)


def _even_out_kernel(z_ref, zp_ref, zn_ref, x_ref, ws_ref, bs_ref, alg_ref, alb_ref, cw_ref, cb_ref,
                     blg_ref, blb_ref, wo_ref, o_ref, g_scr, gs_scr, cat_scr, *, ts, tiles_per_seq):
    pos = pl.program_id(0) % tiles_per_seq
    keep_prev = jnp.where(pos == 0, 0.0, 1.0)
    keep_next = jnp.where(pos == tiles_per_seq - 1, 0.0, 1.0)

    def glu(zz):
        zz = zz.astype(F32)
        return zz[:, :B_WIDTH] * _sigmoid(zz[:, B_WIDTH:])

    g_scr[CONV_HALO:CONV_HALO + ts, :] = glu(z_ref[:, 2 * A_WIDTH:])
    g_scr[0:CONV_HALO, :] = glu(zp_ref[...]) * keep_prev
    g_scr[CONV_HALO + ts:2 * CONV_HALO + ts, :] = glu(zn_ref[...]) * keep_next

    span = ts + 2 * CONV_HALO - 8
    for s in range(1, 8):
        gs_scr[s - 1] = g_scr[s:s + span, :]

    first_tap = CONV_HALO - CONV_W // 2
    for r0 in range(0, ts, CONV_ROWS):
        acc = jnp.zeros((CONV_ROWS, B_WIDTH), F32)
        for k in range(CONV_W):
            a, s = divmod(first_tap + k, 8)
            src = g_scr if s == 0 else gs_scr.at[s - 1]
            acc = acc + cw_ref[k:k + 1, :] * src[r0 + 8 * a:r0 + 8 * a + CONV_ROWS, :]
        c = _layernorm(acc + cb_ref[...], blg_ref[...], blb_ref[...])
        cat_scr[r0:r0 + CONV_ROWS, A_WIDTH:] = (c * _sigmoid(c)).astype(BF16)

    for r0 in range(0, ts, SG_CHUNK):
        zz = z_ref[r0:r0 + SG_CHUNK, 0:2 * A_WIDTH].astype(F32)
        u = _gelu_tanh(zz[:, :A_WIDTH])
        v = _layernorm(_gelu_tanh(zz[:, A_WIDTH:]), alg_ref[...], alb_ref[...]).astype(BF16)
        for g in range(A_GROUPS):
            cs = slice(g * GROUP_DIM, (g + 1) * GROUP_DIM)
            mixed = _bdot(ws_ref[g], v[:, cs]) + bs_ref[:, g:g + 1]
            cat_scr[r0:r0 + SG_CHUNK, cs] = (u[:, cs] * mixed).astype(BF16)

    o_ref[...] = x_ref[...] + _bdot(cat_scr[...], wo_ref[...])


def _even_out(z, x, seq, ws, bs_t, alg, alb, cw, cb, blg, blb, wo):
    t = x.shape[0]
    ts = min(ROW_TILE, seq)
    tiles_per_seq = seq // ts
    hb = ts // CONV_HALO
    n_halo = t // CONV_HALO
    return pl.pallas_call(
        functools.partial(_even_out_kernel, ts=ts, tiles_per_seq=tiles_per_seq),
        grid=(t // ts,),
        in_specs=[
            pl.BlockSpec((ts, 2 * D), lambda i: (i, 0)),
            pl.BlockSpec((CONV_HALO, D), lambda i: (jnp.maximum(i * hb - 1, 0), 1)),
            pl.BlockSpec((CONV_HALO, D), lambda i: (jnp.minimum((i + 1) * hb, n_halo - 1), 1)),
            pl.BlockSpec((ts, D), lambda i: (i, 0)),
            _full((A_GROUPS, SG_CHUNK, SG_CHUNK)), _full((SG_CHUNK, A_GROUPS)),
            _full((1, A_WIDTH)), _full((1, A_WIDTH)),
            _full((CONV_W, B_WIDTH)), _full((1, B_WIDTH)), _full((1, B_WIDTH)), _full((1, B_WIDTH)),
            _full((D, D)),
        ],
        out_specs=pl.BlockSpec((ts, D), lambda i: (i, 0)),
        out_shape=jax.ShapeDtypeStruct((t, D), F32),
        scratch_shapes=[pltpu.VMEM((ts + 2 * CONV_HALO, B_WIDTH), F32),
                        pltpu.VMEM((7, ts + 2 * CONV_HALO - 8, B_WIDTH), F32),
                        pltpu.VMEM((ts, D), BF16)],
        compiler_params=_params("parallel"),
        name="even_out",
    )(z, z, z, x, ws, bs_t, alg, alb, cw, cb, blg, blb, wo)


def _odd_in_kernel(x_ref, g_ref, wk_ref, wt_ref, bgt_ref, k_ref, zt_ref, gr_ref):
    h = _rms(x_ref[...], g_ref[...])
    hb = h.astype(BF16)
    h_lo = (h - hb.astype(F32)).astype(BF16)
    k_ref[...] = _bdot(hb, wk_ref[...]).astype(BF16)
    n = zt_ref.shape[0]
    for j in range(0, n - 1024, 1024):
        zt_ref[j:j + 1024, :] = lax.dot_general(wt_ref[j:j + 1024, :], hb, NT_DIMS,
                                                preferred_element_type=F32).astype(BF16)
    tail = lax.dot_general(wt_ref[n - 1024:, :], hb, NT_DIMS, preferred_element_type=F32)
    zt_ref[n - 1024:, :] = tail[0:1024].astype(BF16)
    corr = lax.dot_general(wt_ref[n:n + N_GATES, :], h_lo, NT_DIMS, preferred_element_type=F32)
    gr_ref[...] = tail[1024:1024 + N_GATES] + tail[1024 + N_GATES:] + corr + bgt_ref[...]


def _odd_in(x, g, wk, wt, bgt, tm):
    t, n = x.shape[0], wt.shape[0] - 2 * N_GATES
    return pl.pallas_call(
        _odd_in_kernel,
        grid=(t // tm,),
        in_specs=[pl.BlockSpec((tm, D), lambda i: (i, 0)), _full((1, D)), _full((D, D)),
                  _full((n + 2 * N_GATES, D)), _full((N_GATES, 1))],
        out_specs=[pl.BlockSpec((tm, D), lambda i: (i, 0)),
                   pl.BlockSpec((n, tm), lambda i: (0, i)),
                   pl.BlockSpec((N_GATES, tm), lambda i: (0, i))],
        out_shape=[jax.ShapeDtypeStruct((t, D), BF16),
                   jax.ShapeDtypeStruct((n, t), BF16),
                   jax.ShapeDtypeStruct((N_GATES, t), F32)],
        compiler_params=_params("parallel"),
        name="odd_in",
    )(x, g, wk, wt, bgt)


def _mlstm_kernel(qf_ref, kf_ref, vf_ref, qb_ref, kb_ref, vb_ref, grf_ref, grb_ref,
                  hf_ref, hb_ref, c_scr, n_scr, m_scr):
    @pl.when(pl.program_id(1) == 0)
    def _():
        c_scr[...] = jnp.zeros_like(c_scr)
        n_scr[...] = jnp.zeros_like(n_scr)
        m_scr[...] = jnp.zeros_like(m_scr)

    L = LSTM_CHUNK
    row = lax.broadcasted_iota(I32, (L, L), 0)
    col = lax.broadcasted_iota(I32, (L, L), 1)
    lower = row >= col
    upper = row <= col
    lower_b = jnp.where(lower, 1.0, 0.0).astype(BF16)
    upper_b = jnp.where(upper, 1.0, 0.0).astype(BF16)
    eye_b = jnp.where(row == col, 1.0, 0.0).astype(BF16)

    def split3(x):
        hi = x.astype(BF16)
        r = x - hi.astype(F32)
        mid = r.astype(BF16)
        return hi, mid, (r - mid.astype(F32)).astype(BF16)

    dirs = ((qf_ref, kf_ref, vf_ref, grf_ref, hf_ref),
            (qb_ref, kb_ref, vb_ref, grb_ref, hb_ref))
    for d, (q_ref, k_ref, v_ref, gr_ref, h_ref) in enumerate(dirs):
        grow = gr_ref[...]
        if d == 0:
            tri, mask_t, last = upper_b, upper, L - 1
        else:
            tri, mask_t, last = lower_b, lower, 0
        b_rows = sum(_bdot(p, tri) for p in split3(_log_sigmoid(grow)))
        src_rows = grow - pltpu.roll(b_rows, N_GATES - HEADS, 0)
        src_cols = sum(lax.dot_general(eye_b, p, NT_DIMS, preferred_element_type=F32) for p in split3(src_rows))
        i_off, f_off = 8 * d, 8 * d + HEADS
        k_scale = HEAD_DIM ** -0.5
        for hh in range(HEADS):
            idx = d * HEADS + hh
            cs = slice(hh * HEAD_DIM, (hh + 1) * HEAD_DIM)
            k = k_ref[:, cs]
            qt = q_ref[cs, :]
            vt = v_ref[cs, :]
            b_row = b_rows[f_off + hh:f_off + hh + 1, :]
            i_row = grow[i_off + hh:i_off + hh + 1, :]
            src_col = src_cols[:, i_off + hh:i_off + hh + 1]
            m = m_scr[idx][:, 0:1]
            cmat = c_scr[idx]
            nrows = n_scr[idx]

            dmat_t = jnp.where(mask_t, b_row + src_col, -jnp.inf)
            inter = b_row + m
            m_row = jnp.maximum(jnp.max(dmat_t, axis=0, keepdims=True), inter)
            smat_t = _bdot(k, qt) * (jnp.exp(dmat_t - m_row) * k_scale)
            w_inter = jnp.exp(inter - m_row)
            num_t = _bdot(vt, smat_t.astype(BF16)) + w_inter * _bdot(cmat.astype(BF16), qt)
            den = (jnp.sum(smat_t, axis=0, keepdims=True)
                   + w_inter * _bdot(nrows.astype(BF16), qt)[0:1, :])
            rden = 1.0 / jnp.maximum(jnp.abs(den), jnp.exp(-m_row))
            h_ref[cs, :] = (num_t * rden).astype(h_ref.dtype)

            b_last = b_row[:, last:last + 1]
            g_row = b_last - b_row + i_row
            m_new = jnp.maximum(b_last + m, jnp.max(g_row, axis=1, keepdims=True))
            wg = jnp.exp(g_row - m_new) * k_scale
            decay = jnp.exp(b_last + m - m_new)
            c_scr[idx] = decay * cmat + _bdot((vt.astype(F32) * wg).astype(BF16), k)
            n_scr[idx] = decay * nrows + _bdot(jnp.broadcast_to(wg, (8, L)).astype(BF16), k)
            m_scr[idx] = jnp.broadcast_to(m_new, (1, 128))


def _mlstm(k, zt, gr, batch, seq):
    t = k.shape[0]
    nc = seq // LSTM_CHUNK
    L = LSTM_CHUNK
    fwd = lambda b, c: (b * nc + c, 0)
    bwd = lambda b, c: (b * nc + nc - 1 - c, 0)

    def fwd_t(rb):
        return lambda b, c: (rb, b * nc + c)

    def bwd_t(rb):
        return lambda b, c: (rb, b * nc + nc - 1 - c)

    qkv = lambda f, ft: [pl.BlockSpec((D, L), ft(0)), pl.BlockSpec((L, D), f), pl.BlockSpec((D, L), ft(1))]
    return pl.pallas_call(
        _mlstm_kernel,
        grid=(batch, nc),
        in_specs=qkv(fwd, fwd_t) + qkv(bwd, bwd_t) + [
            pl.BlockSpec((N_GATES, L), fwd_t(0)), pl.BlockSpec((N_GATES, L), bwd_t(0)),
        ],
        out_specs=[pl.BlockSpec((D, L), fwd_t(0)), pl.BlockSpec((D, L), bwd_t(0))],
        out_shape=[jax.ShapeDtypeStruct((D, t), BF16), jax.ShapeDtypeStruct((D, t), BF16)],
        scratch_shapes=[pltpu.VMEM((2 * HEADS, HEAD_DIM, HEAD_DIM), F32),
                        pltpu.VMEM((2 * HEADS, 8, HEAD_DIM), F32),
                        pltpu.VMEM((2 * HEADS, 1, 128), F32)],
        compiler_params=_params("parallel", "arbitrary"),
        name="mlstm",
    )(zt, k, zt, zt, k, zt, gr, gr)


def _odd_out_kernel(hf_ref, hb_ref, og_ref, x_ref, hg_ref, wo_ref, o_ref, cat_scr):
    tm = x_ref.shape[0]
    for hh in range(HEADS):
        rs = slice(hh * HEAD_DIM, (hh + 1) * HEAD_DIM)
        s = hf_ref[rs, :].astype(F32) + hb_ref[rs, :].astype(F32)
        inv = lax.rsqrt(jnp.mean(s * s, axis=0, keepdims=True) + EPS)
        gate = _sigmoid(og_ref[rs, :].astype(F32))
        for c0 in range(0, tm, 128):
            cs = slice(c0, c0 + 128)
            cat_scr[rs, cs] = (gate[:, cs] * (s[:, cs] * inv[:, cs]) * hg_ref[rs, :]).astype(BF16)
    o_ref[...] = x_ref[...] + lax.dot_general(cat_scr[...], wo_ref[...], TN_DIMS, preferred_element_type=F32)


def _odd_out(hf, hb, zt, x, hg_rep, wo, tm):
    t = x.shape[0]
    col = lambda i: (0, i)
    return pl.pallas_call(
        _odd_out_kernel,
        grid=(t // tm,),
        in_specs=[pl.BlockSpec((D, tm), col), pl.BlockSpec((D, tm), col),
                  pl.BlockSpec((D, tm), lambda i: (2, i)), pl.BlockSpec((tm, D), lambda i: (i, 0)),
                  _full((D, 128)), _full((D, D))],
        out_specs=pl.BlockSpec((tm, D), lambda i: (i, 0)),
        out_shape=jax.ShapeDtypeStruct((t, D), F32),
        scratch_shapes=[pltpu.VMEM((D, tm), BF16)],
        compiler_params=_params("parallel"),
        name="odd_out",
    )(hf, hb, zt, x, hg_rep, wo)


def _xattn_kernel(x_ref, g_ref, kv_ref, wq_ref, wo_ref, gr_ref, wr_ref, o_ref, aff_ref, cat_scr):
    x = x_ref[...]
    h = _rms(x, g_ref[...]).astype(BF16)
    q = (_bdot(h, wq_ref[...]) * (HEAD_DIM ** -0.5)).astype(BF16)
    for hh in range(HEADS):
        cs = slice(hh * HEAD_DIM, (hh + 1) * HEAD_DIM)
        k = kv_ref[:, cs]
        v = kv_ref[:, D + hh * HEAD_DIM:D + (hh + 1) * HEAD_DIM]
        s = lax.dot_general(q[:, cs], k, NT_DIMS, preferred_element_type=F32)
        p = jnp.exp(s - jnp.max(s, axis=-1, keepdims=True))
        p = p / jnp.sum(p, axis=-1, keepdims=True)
        cat_scr[:, cs] = _bdot(p.astype(BF16), v).astype(BF16)
    y = x + _bdot(cat_scr[...], wo_ref[...])
    o_ref[...] = y

    hr = _rms(y, gr_ref[...])
    hr_hi = hr.astype(BF16)
    hr_lo = (hr - hr_hi.astype(F32)).astype(BF16)
    main = lax.dot_general(wr_ref[...], hr_hi, NT_DIMS, preferred_element_type=F32)
    corr = lax.dot_general(wr_ref[0:N_EXPERTS, :], hr_lo, NT_DIMS, preferred_element_type=F32)
    logits = main[0:N_EXPERTS] + main[N_EXPERTS:] + corr
    p = jnp.exp(logits - jnp.max(logits, axis=0, keepdims=True))
    aff_ref[...] = p / jnp.sum(p, axis=0, keepdims=True)


def _xattn(x, g, kv, wq, wo, g_ffn, wr, seq):
    t = x.shape[0]
    ts = min(ROW_TILE, seq)
    tiles_per_seq = seq // ts
    return pl.pallas_call(
        _xattn_kernel,
        grid=(t // ts,),
        in_specs=[pl.BlockSpec((ts, D), lambda i: (i, 0)), _full((1, D)),
                  pl.BlockSpec((MEM_LEN, 2 * D), lambda i: (i // tiles_per_seq, 0)),
                  _full((D, D)), _full((D, D)), _full((1, D)), _full((2 * N_EXPERTS, D))],
        out_specs=[pl.BlockSpec((ts, D), lambda i: (i, 0)), pl.BlockSpec((N_EXPERTS, ts), lambda i: (0, i))],
        out_shape=[jax.ShapeDtypeStruct((t, D), F32), jax.ShapeDtypeStruct((N_EXPERTS, t), F32)],
        scratch_shapes=[pltpu.VMEM((ts, D), BF16)],
        compiler_params=_params("parallel"),
        name="xattn",
    )(x, g, kv, wq, wo, g_ffn, wr)


def _select_kernel(aff_ref, rank_ref, p0_ref, n_ref, *, cap, nblk):
    a = aff_ref[0]
    bits = pltpu.bitcast(a, I32)
    capf = float(cap)

    def count(m):
        return jnp.sum(jnp.where(m, 1.0, 0.0), keepdims=True)

    def bisect(_, carry):
        lo, hi = carry
        mid = lo + ((hi - lo + 1) >> 1)
        ok = count(bits >= mid) >= capf
        return jnp.where(ok, mid, lo), jnp.where(ok, hi, mid - 1)

    lo, _ = lax.fori_loop(0, 31, bisect,
                          (jnp.zeros((1, 1), I32), jnp.full((1, 1), 0x7F800000, I32)))
    gt = bits > lo
    eq = bits == lo
    need = capf - count(gt)

    r = lax.broadcasted_iota(I32, (MOE_BLOCK, MOE_BLOCK), 0)
    c = lax.broadcasted_iota(I32, (MOE_BLOCK, MOE_BLOCK), 1)
    incl_mat = jnp.where(r <= c, 1.0, 0.0).astype(BF16)
    rb = lax.broadcasted_iota(I32, (nblk, nblk), 0)
    cb = lax.broadcasted_iota(I32, (nblk, nblk), 1)
    before = jnp.where(rb > cb, 1.0, 0.0)

    def block_offsets(tot):
        return jnp.dot(before, jnp.broadcast_to(tot, (nblk, 128)), precision=HIGHEST,
                       preferred_element_type=F32)

    incl_eq = _bdot(jnp.where(eq, 1.0, 0.0).astype(BF16), incl_mat)
    rank_eq = block_offsets(incl_eq[:, MOE_BLOCK - 1:MOE_BLOCK])[:, 0:1] + incl_eq - 1.0
    sel = jnp.logical_or(gt, jnp.logical_and(eq, rank_eq < need))

    incl = _bdot(jnp.where(sel, 1.0, 0.0).astype(BF16), incl_mat)
    n = incl[:, MOE_BLOCK - 1:MOE_BLOCK]
    n_aligned = jnp.floor((n + (SLOT_ALIGN - 1)) / SLOT_ALIGN) * SLOT_ALIGN
    rank_ref[0] = jnp.where(sel, incl - 1.0, -1.0).astype(I32)
    p0_ref[0] = block_offsets(n_aligned).astype(I32)
    n_ref[0] = jnp.broadcast_to(n, (nblk, 128)).astype(I32)


def _select(aff3, cap):
    _, nblk, _ = aff3.shape
    blk = lambda w: pl.BlockSpec((1, nblk, w), lambda e: (e, 0, 0))
    return pl.pallas_call(
        functools.partial(_select_kernel, cap=cap, nblk=nblk),
        grid=(N_EXPERTS,),
        in_specs=[blk(MOE_BLOCK)],
        out_specs=[blk(MOE_BLOCK), blk(128), blk(128)],
        out_shape=[jax.ShapeDtypeStruct((N_EXPERTS, nblk, MOE_BLOCK), I32),
                   jax.ShapeDtypeStruct((N_EXPERTS, nblk, 128), I32),
                   jax.ShapeDtypeStruct((N_EXPERTS, nblk, 128), I32)],
        compiler_params=_params("parallel"),
        name="select",
    )(aff3)


def _pack_kernel(p0_ref, n_ref, pend_ref, x_ref, g_ref, rank_ref, xe_ref, h_scr, p_scr, stage, extra,
                 sem, esem, *, nblk, cap, cap_x):
    b = pl.program_id(0)
    slot = b % 2

    def copy(e, start, slot):
        return pltpu.make_async_copy(stage.at[slot, pl.ds(e * SLOT_WINDOW, SLOT_WINDOW), :],
                                     xe_ref.at[e, pl.ds(start, SLOT_WINDOW), :], sem.at[e])

    def extra_copy(e, start):
        return pltpu.make_async_copy(extra, xe_ref.at[e, pl.ds(start, SLOT_WINDOW), :], esem.at[0])

    def window_start(e, w):
        return pl.multiple_of(p0_ref[e, b] + SLOT_WINDOW * w, SLOT_ALIGN)

    h_scr[...] = _rms(x_ref[...], g_ref[...]).astype(BF16)
    slot_id = lax.broadcasted_iota(I32, (SLOT_WINDOW, MOE_BLOCK), 0)
    for e in range(N_EXPERTS):
        p_scr[e * SLOT_WINDOW:(e + 1) * SLOT_WINDOW, :] = jnp.where(
            rank_ref[e, 0] == slot_id, 1.0, 0.0).astype(BF16)
    for r0 in range(0, N_EXPERTS * SLOT_WINDOW, MOE_BLOCK):
        stage[slot, r0:r0 + MOE_BLOCK, :] = _bdot(p_scr[r0:r0 + MOE_BLOCK, :], h_scr[...]).astype(BF16)

    for e in range(N_EXPERTS):
        @pl.when(b > 0)
        def _(e=e):
            copy(e, 0, 1 - slot).wait()
        copy(e, window_start(e, 0), slot).start()
        for w in range(1, MAX_WINDOWS):
            @pl.when(n_ref[e, b] > SLOT_WINDOW * w)
            def _(e=e, w=w):
                onehot = jnp.where(rank_ref[e, 0] - SLOT_WINDOW * w == slot_id, 1.0, 0.0).astype(BF16)
                extra[...] = _bdot(onehot, h_scr[...]).astype(BF16)
                c = extra_copy(e, window_start(e, w))
                c.start()
                c.wait()

    @pl.when(b == nblk - 1)
    def _():
        for e in range(N_EXPERTS):
            copy(e, 0, slot).wait()
        extra[...] = jnp.zeros_like(extra)
        n_zero = -(-(cap_x - cap) // SLOT_WINDOW)
        last_start = cap_x - SLOT_WINDOW
        for phase in ("start", "wait"):
            for e in range(N_EXPERTS):
                for z in range(n_zero):
                    start = pend_ref[e] + SLOT_WINDOW * z

                    @pl.when(start < last_start)
                    def _(e=e, start=start, phase=phase):
                        c = extra_copy(e, pl.multiple_of(start, SLOT_ALIGN))
                        c.start() if phase == "start" else c.wait()
        for e in range(N_EXPERTS):
            extra_copy(e, last_start).start()
        for e in range(N_EXPERTS):
            extra_copy(e, last_start).wait()


def _pack(p0, n, p_end, x, g, rank4, cap_x):
    t = x.shape[0]
    nblk = t // MOE_BLOCK
    grid_spec = pltpu.PrefetchScalarGridSpec(
        num_scalar_prefetch=3,
        grid=(nblk,),
        in_specs=[pl.BlockSpec((MOE_BLOCK, D), lambda b, *_: (b, 0)),
                  pl.BlockSpec((1, D), lambda b, *_: (0, 0)),
                  pl.BlockSpec((N_EXPERTS, 1, 1, MOE_BLOCK), lambda b, *_: (0, b, 0, 0))],
        out_specs=pl.BlockSpec(memory_space=pl.ANY),
        scratch_shapes=[pltpu.VMEM((MOE_BLOCK, D), BF16),
                        pltpu.VMEM((N_EXPERTS * SLOT_WINDOW, MOE_BLOCK), BF16),
                        pltpu.VMEM((2, N_EXPERTS * SLOT_WINDOW, D), BF16),
                        pltpu.VMEM((SLOT_WINDOW, D), BF16),
                        pltpu.SemaphoreType.DMA((N_EXPERTS,)),
                        pltpu.SemaphoreType.DMA((1,))],
    )
    return pl.pallas_call(
        functools.partial(_pack_kernel, nblk=nblk, cap=CAPACITY_FACTOR * t // N_EXPERTS, cap_x=cap_x),
        grid_spec=grid_spec,
        out_shape=jax.ShapeDtypeStruct((N_EXPERTS, cap_x, D), BF16),
        compiler_params=_params("arbitrary"),
        name="pack",
    )(p0, n, p_end, x, g, rank4)


def _ffn_kernel(pend_ref, xe_ref, w1_ref, w3_ref, w2_ref, ye_ref, *, tm):
    e, i = pl.program_id(0), pl.program_id(1)
    active = i * tm < pend_ref[e]

    @pl.when(active)
    def _():
        xb = xe_ref[0]
        acc = jnp.zeros((tm, D), F32)
        for f in range(0, EXPERT_FF, FF_CHUNK):
            h1 = _bdot(xb, w1_ref[0, :, f:f + FF_CHUNK])
            h3 = _bdot(xb, w3_ref[0, :, f:f + FF_CHUNK])
            hid = (h1 * _sigmoid(h1) * h3).astype(BF16)
            acc = acc + _bdot(hid, w2_ref[0, f:f + FF_CHUNK, :])
        ye_ref[0] = acc.astype(BF16)

    @pl.when(jnp.logical_not(active))
    def _():
        ye_ref[0] = jnp.zeros((tm, D), BF16)


def _ffn(p_end, xe, w1, w3, w2):
    _, cap_x, _ = xe.shape
    tm = FFN_TILE
    grid_spec = pltpu.PrefetchScalarGridSpec(
        num_scalar_prefetch=1,
        grid=(N_EXPERTS, cap_x // tm),
        in_specs=[pl.BlockSpec((1, tm, D), lambda e, i, *_: (e, i, 0)),
                  pl.BlockSpec((1, D, EXPERT_FF), lambda e, i, *_: (e, 0, 0)),
                  pl.BlockSpec((1, D, EXPERT_FF), lambda e, i, *_: (e, 0, 0)),
                  pl.BlockSpec((1, EXPERT_FF, D), lambda e, i, *_: (e, 0, 0))],
        out_specs=pl.BlockSpec((1, tm, D), lambda e, i, *_: (e, i, 0)),
    )
    return pl.pallas_call(
        functools.partial(_ffn_kernel, tm=tm),
        grid_spec=grid_spec,
        out_shape=jax.ShapeDtypeStruct(xe.shape, BF16),
        compiler_params=_params("parallel", "arbitrary"),
        name="ffn",
    )(p_end, xe, w1, w3, w2)


def _unpack_kernel(p0_ref, n_ref, x_ref, rank_ref, gate_ref, gfin_ref, ye_ref, o_ref, slab, u_scr,
                   extra, sem, esem, *, nblk, final_norm):
    b = pl.program_id(0)
    slot = b % 2
    rows = N_EXPERTS * SLOT_WINDOW

    def copy(e, blk, slot):
        start = pl.multiple_of(p0_ref[e, blk], SLOT_ALIGN)
        return pltpu.make_async_copy(ye_ref.at[e, pl.ds(start, SLOT_WINDOW), :],
                                     slab.at[slot, pl.ds(e * SLOT_WINDOW, SLOT_WINDOW), :], sem.at[slot])

    @pl.when(b == 0)
    def _():
        for e in range(N_EXPERTS):
            copy(e, 0, 0).start()

    @pl.when(b + 1 < nblk)
    def _():
        for e in range(N_EXPERTS):
            copy(e, b + 1, 1 - slot).start()

    lane = lax.broadcasted_iota(I32, (MOE_BLOCK, SLOT_WINDOW), 1)
    for e in range(N_EXPERTS):
        u_scr[:, e * SLOT_WINDOW:(e + 1) * SLOT_WINDOW] = jnp.where(
            rank_ref[:, e:e + 1] == lane, gate_ref[:, e:e + 1], 0.0).astype(BF16)
    for e in range(N_EXPERTS):
        copy(e, b, slot).wait()
    o_ref[...] = x_ref[...] + _bdot(u_scr[...], slab[slot])

    for e in range(N_EXPERTS):
        for w in range(1, MAX_WINDOWS):
            @pl.when(n_ref[e, b] > SLOT_WINDOW * w)
            def _(e=e, w=w):
                start = pl.multiple_of(p0_ref[e, b] + SLOT_WINDOW * w, SLOT_ALIGN)
                c = pltpu.make_async_copy(ye_ref.at[e, pl.ds(start, SLOT_WINDOW), :], extra, esem.at[0])
                c.start()
                c.wait()
                onehot = jnp.where(rank_ref[:, e:e + 1] - SLOT_WINDOW * w == lane, 1.0, 0.0).astype(BF16)
                o_ref[...] += gate_ref[:, e:e + 1] * _bdot(onehot, extra[...])

    if final_norm:
        o_ref[...] = _rms(o_ref[...], gfin_ref[...])


def _unpack(p0, n, x, rank_t, gate_t, gfin, ye, final_norm):
    t = x.shape[0]
    nblk = t // MOE_BLOCK
    rows = N_EXPERTS * SLOT_WINDOW
    grid_spec = pltpu.PrefetchScalarGridSpec(
        num_scalar_prefetch=2,
        grid=(nblk,),
        in_specs=[pl.BlockSpec((MOE_BLOCK, D), lambda b, *_: (b, 0)),
                  pl.BlockSpec((MOE_BLOCK, N_EXPERTS), lambda b, *_: (b, 0)),
                  pl.BlockSpec((MOE_BLOCK, N_EXPERTS), lambda b, *_: (b, 0)),
                  pl.BlockSpec((1, D), lambda b, *_: (0, 0)),
                  pl.BlockSpec(memory_space=pl.ANY)],
        out_specs=pl.BlockSpec((MOE_BLOCK, D), lambda b, *_: (b, 0)),
        scratch_shapes=[pltpu.VMEM((2, rows, D), BF16),
                        pltpu.VMEM((MOE_BLOCK, rows), BF16),
                        pltpu.VMEM((SLOT_WINDOW, D), BF16),
                        pltpu.SemaphoreType.DMA((2,)),
                        pltpu.SemaphoreType.DMA((1,))],
    )
    return pl.pallas_call(
        functools.partial(_unpack_kernel, nblk=nblk, final_norm=final_norm),
        grid_spec=grid_spec,
        out_shape=jax.ShapeDtypeStruct((t, D), F32),
        compiler_params=_params("arbitrary"),
        name="unpack",
    )(p0, n, x, rank_t, gate_t, gfin, ye)


def _moe(x, aff_t, g, w1, w3, w2, gfin, final_norm):
    t = x.shape[0]
    nblk = t // MOE_BLOCK
    cap = CAPACITY_FACTOR * t // N_EXPERTS
    cap_x = -(-(cap + SLOT_ALIGN * nblk + FFN_TILE) // FFN_TILE) * FFN_TILE

    rank, p0w, nw = _select(aff_t.reshape(N_EXPERTS, nblk, MOE_BLOCK), cap)
    p0, n = p0w[:, :, 0], nw[:, :, 0]
    p_end = p0[:, -1] + (n[:, -1] + SLOT_ALIGN - 1) // SLOT_ALIGN * SLOT_ALIGN

    xe = _pack(p0, n, p_end, x, g, rank.reshape(N_EXPERTS, nblk, 1, MOE_BLOCK), cap_x)
    ye = _ffn(p_end, xe, w1, w3, w2)
    return _unpack(p0, n, x, rank.reshape(N_EXPERTS, t).T, aff_t.T, gfin, ye, final_norm)


def _trunk(x3, mem3, w):
    batch, seq, _ = x3.shape
    t = batch * seq
    tm = min(ROW_TILE, seq)
    x = x3.reshape(t, D)
    mem = mem3.reshape(batch * MEM_LEN, D)
    row = lambda v: v.reshape(1, -1)
    depth = w["norm_mix"].shape[0]
    for l in range(depth):
        j = l // 2
        g_mix = row(w["norm_mix"][l])
        if l % 2 == 0:
            z = _proj(x, g_mix, w["even_w_in"][j], norm=True, tm=tm)
            x = _even_out(z, x, seq, w["a_ws"][j], w["a_bs"][j].T, row(w["a_ln_g"][j]), row(w["a_ln_b"][j]),
                          w["b_conv_w"][j], row(w["b_conv_b"][j]), row(w["b_ln_g"][j]), row(w["b_ln_b"][j]),
                          w["even_w_out"][j])
        else:
            k, zt, gr = _odd_in(x, g_mix, w["odd_w_k"][j], w["odd_w_t"][j], w["odd_b_gate"][j].reshape(-1, 1), tm)
            hf, hb = _mlstm(k, zt, gr, batch, seq)
            hg_rep = jnp.broadcast_to(w["odd_hnorm_g"][j].reshape(D, 1), (D, 128))
            x = _odd_out(hf, hb, zt, x, hg_rep, w["odd_w_out"][j], tm)
        kv = _proj(mem, g_mix, w["xa_wkv"][l], norm=False, tm=MEM_LEN)
        g_ffn = row(w["norm_ffn"][l])
        x, aff_t = _xattn(x, row(w["norm_xattn"][l]), kv, w["xa_wq"][l], w["xa_wo"][l], g_ffn,
                          w["moe_router_t"][l], seq)
        x = _moe(x, aff_t, g_ffn, w["moe_w1"][l], w["moe_w3"][l], w["moe_w2"][l], row(w["norm_final"]),
                 final_norm=(l == depth - 1))
    return x.reshape(batch, seq, D)


def _prepare_weights(norm_mix, norm_xattn, norm_ffn, even_w_in, a_ws, a_bs, a_ln_g, a_ln_b, b_conv_w, b_conv_b,
                     b_ln_g, b_ln_b, even_w_out, odd_w_in, odd_b_gate, odd_hnorm_g, odd_w_out, xa_wq, xa_wkv,
                     xa_wo, moe_router, moe_w1, moe_w3, moe_w2, norm_final):
    bf = lambda v: v.astype(BF16)
    router_t = jnp.swapaxes(moe_router, 1, 2)
    router_hi = bf(router_t)
    router_lo = bf(router_t - router_hi.astype(F32))
    gate_t = jnp.swapaxes(odd_w_in[:, :, 4 * D:], 1, 2)
    gate_hi = bf(gate_t)
    gate_lo = bf(gate_t - gate_hi.astype(F32))
    w = dict(
        norm_mix=norm_mix, norm_xattn=norm_xattn, norm_ffn=norm_ffn, norm_final=norm_final,
        even_w_in=bf(even_w_in), a_ws=bf(a_ws), a_bs=a_bs, a_ln_g=a_ln_g, a_ln_b=a_ln_b,
        b_conv_w=b_conv_w, b_conv_b=b_conv_b, b_ln_g=b_ln_g, b_ln_b=b_ln_b, even_w_out=bf(even_w_out),
        odd_w_t=jnp.concatenate([bf(jnp.swapaxes(odd_w_in[:, :, 0:D], 1, 2)),
                                 bf(jnp.swapaxes(odd_w_in[:, :, 2 * D:4 * D], 1, 2)), gate_hi, gate_lo], axis=1),
        odd_w_k=bf(odd_w_in[:, :, D:2 * D]), odd_b_gate=odd_b_gate,
        odd_hnorm_g=odd_hnorm_g, odd_w_out=bf(odd_w_out),
        xa_wq=bf(xa_wq), xa_wkv=bf(xa_wkv), xa_wo=bf(xa_wo),
        moe_router_t=jnp.concatenate([router_hi, router_lo], axis=1),
        moe_w1=bf(moe_w1), moe_w3=bf(moe_w3), moe_w2=bf(moe_w2),
    )
    return w


def kernel(x_prompt, x_sample, mem_prompt, mem_sample, norm_mix, norm_xattn, norm_ffn, even_w_in, a_ws, a_bs, a_ln_g, a_ln_b, b_conv_w, b_conv_b, b_ln_g, b_ln_b, even_w_out, odd_w_in, odd_b_gate, odd_hnorm_g, odd_w_out, xa_wq, xa_wkv, xa_wo, moe_router, moe_w1, moe_w3, moe_w2, norm_final):
    w = _prepare_weights(norm_mix, norm_xattn, norm_ffn, even_w_in, a_ws, a_bs, a_ln_g, a_ln_b, b_conv_w,
                         b_conv_b, b_ln_g, b_ln_b, even_w_out, odd_w_in, odd_b_gate, odd_hnorm_g, odd_w_out,
                         xa_wq, xa_wkv, xa_wo, moe_router, moe_w1, moe_w3, moe_w2, norm_final)
    return _trunk(x_prompt, mem_prompt, w), _trunk(x_sample, mem_sample, w)
```

```python
import functools

import jax
import jax.numpy as jnp
from jax import lax
from jax.experimental import pallas as pl
from jax.experimental.pallas import tpu as pltpu

F32 = jnp.float32
BF16 = jnp.bfloat16
I32 = jnp.int32
EPS = 1e-6
HIGHEST = lax.Precision.HIGHEST

D = 1024
ROW_TILE = 512
A_WIDTH = 512
A_GROUPS = 4
GROUP_DIM = 128
SG_CHUNK = 128
B_WIDTH = 512
CONV_W = 31
CONV_HALO = 16
CONV_ROWS = 64
HEADS = 4
HEAD_DIM = 256
LSTM_CHUNK = 128
N_GATES = 16
MEM_LEN = 256
N_EXPERTS = 16
EXPERT_FF = 2048
FF_CHUNK = 512
CAPACITY_FACTOR = 2
MOE_BLOCK = 512
SLOT_WINDOW = 128
MAX_WINDOWS = MOE_BLOCK // SLOT_WINDOW
SLOT_ALIGN = 16
FFN_TILE = 512
VMEM_LIMIT = 56 * 1024 * 1024

NT_DIMS = (((1,), (1,)), ((), ()))
TN_DIMS = (((0,), (0,)), ((), ()))


def _params(*sem):
    return pltpu.CompilerParams(dimension_semantics=sem, vmem_limit_bytes=VMEM_LIMIT)


def _rms(x, g):
    return x * lax.rsqrt(jnp.mean(x * x, axis=-1, keepdims=True) + EPS) * g


def _layernorm(x, g, b):
    mu = jnp.mean(x, axis=-1, keepdims=True)
    xc = x - mu
    var = jnp.mean(xc * xc, axis=-1, keepdims=True)
    return xc * lax.rsqrt(var + EPS) * g + b


def _sigmoid(x):
    return 1.0 / (1.0 + jnp.exp(-x))


def _gelu_tanh(x):
    return 0.5 * x * (1.0 + jnp.tanh(0.7978845608028654 * (x + 0.044715 * (x * x * x))))


def _log_sigmoid(x):
    return jnp.minimum(x, 0.0) - jnp.log(1.0 + jnp.exp(-jnp.abs(x)))


def _bdot(a, b):
    return jnp.dot(a, b, preferred_element_type=F32)


def _full(shape):
    return pl.BlockSpec(shape, lambda *_: (0,) * len(shape))


def _proj_kernel(x_ref, g_ref, w_ref, o_ref, *, norm):
    x = x_ref[...]
    h = (_rms(x, g_ref[...]) if norm else x).astype(BF16)
    for j in range(0, w_ref.shape[1], 1024):
        o_ref[:, j:j + 1024] = _bdot(h, w_ref[:, j:j + 1024]).astype(o_ref.dtype)


def _proj(x, g, w, *, norm, tm):
    t, n = x.shape[0], w.shape[1]
    return pl.pallas_call(
        functools.partial(_proj_kernel, norm=norm),
        grid=(t // tm,),
        in_specs=[pl.BlockSpec((tm, D), lambda i: (i, 0)), _full((1, D)), _full((D, n))],
        out_specs=pl.BlockSpec((tm, n), lambda i: (i, 0)),
        out_shape=jax.ShapeDtypeStruct((t, n), BF16),
        compiler_params=_params("parallel"),
        name="proj",
    )(x, g, w)


def _even_out_kernel(z_ref, zp_ref, zn_ref, x_ref, ws_ref, bs_ref, alg_ref, alb_ref, cw_ref, cb_ref,
                     blg_ref, blb_ref, wo_ref, o_ref, g_scr, gs_scr, cat_scr, *, ts, tiles_per_seq):
    pos = pl.program_id(0) % tiles_per_seq
    keep_prev = jnp.where(pos == 0, 0.0, 1.0)
    keep_next = jnp.where(pos == tiles_per_seq - 1, 0.0, 1.0)

    def glu(zz):
        zz = zz.astype(F32)
        return zz[:, :B_WIDTH] * _sigmoid(zz[:, B_WIDTH:])

    g_scr[CONV_HALO:CONV_HALO + ts, :] = glu(z_ref[:, 2 * A_WIDTH:])
    g_scr[0:CONV_HALO, :] = glu(zp_ref[...]) * keep_prev
    g_scr[CONV_HALO + ts:2 * CONV_HALO + ts, :] = glu(zn_ref[...]) * keep_next

    span = ts + 2 * CONV_HALO - 8
    for s in range(1, 8):
        gs_scr[s - 1] = g_scr[s:s + span, :]

    first_tap = CONV_HALO - CONV_W // 2
    for r0 in range(0, ts, CONV_ROWS):
        acc = jnp.zeros((CONV_ROWS, B_WIDTH), F32)
        for k in range(CONV_W):
            a, s = divmod(first_tap + k, 8)
            src = g_scr if s == 0 else gs_scr.at[s - 1]
            acc = acc + cw_ref[k:k + 1, :] * src[r0 + 8 * a:r0 + 8 * a + CONV_ROWS, :]
        c = _layernorm(acc + cb_ref[...], blg_ref[...], blb_ref[...])
        cat_scr[r0:r0 + CONV_ROWS, A_WIDTH:] = (c * _sigmoid(c)).astype(BF16)

    for r0 in range(0, ts, SG_CHUNK):
        zz = z_ref[r0:r0 + SG_CHUNK, 0:2 * A_WIDTH].astype(F32)
        u = _gelu_tanh(zz[:, :A_WIDTH])
        v = _layernorm(_gelu_tanh(zz[:, A_WIDTH:]), alg_ref[...], alb_ref[...]).astype(BF16)
        for g in range(A_GROUPS):
            cs = slice(g * GROUP_DIM, (g + 1) * GROUP_DIM)
            mixed = _bdot(ws_ref[g], v[:, cs]) + bs_ref[:, g:g + 1]
            cat_scr[r0:r0 + SG_CHUNK, cs] = (u[:, cs] * mixed).astype(BF16)

    o_ref[...] = x_ref[...] + _bdot(cat_scr[...], wo_ref[...])


def _even_out(z, x, seq, ws, bs_t, alg, alb, cw, cb, blg, blb, wo):
    t = x.shape[0]
    ts = min(ROW_TILE, seq)
    tiles_per_seq = seq // ts
    hb = ts // CONV_HALO
    n_halo = t // CONV_HALO
    return pl.pallas_call(
        functools.partial(_even_out_kernel, ts=ts, tiles_per_seq=tiles_per_seq),
        grid=(t // ts,),
        in_specs=[
            pl.BlockSpec((ts, 2 * D), lambda i: (i, 0)),
            pl.BlockSpec((CONV_HALO, D), lambda i: (jnp.maximum(i * hb - 1, 0), 1)),
            pl.BlockSpec((CONV_HALO, D), lambda i: (jnp.minimum((i + 1) * hb, n_halo - 1), 1)),
            pl.BlockSpec((ts, D), lambda i: (i, 0)),
            _full((A_GROUPS, SG_CHUNK, SG_CHUNK)), _full((SG_CHUNK, A_GROUPS)),
            _full((1, A_WIDTH)), _full((1, A_WIDTH)),
            _full((CONV_W, B_WIDTH)), _full((1, B_WIDTH)), _full((1, B_WIDTH)), _full((1, B_WIDTH)),
            _full((D, D)),
        ],
        out_specs=pl.BlockSpec((ts, D), lambda i: (i, 0)),
        out_shape=jax.ShapeDtypeStruct((t, D), F32),
        scratch_shapes=[pltpu.VMEM((ts + 2 * CONV_HALO, B_WIDTH), F32),
                        pltpu.VMEM((7, ts + 2 * CONV_HALO - 8, B_WIDTH), F32),
                        pltpu.VMEM((ts, D), BF16)],
        compiler_params=_params("parallel"),
        name="even_out",
    )(z, z, z, x, ws, bs_t, alg, alb, cw, cb, blg, blb, wo)


def _odd_in_kernel(x_ref, g_ref, wk_ref, wt_ref, bgt_ref, k_ref, zt_ref, gr_ref):
    h = _rms(x_ref[...], g_ref[...])
    hb = h.astype(BF16)
    h_lo = (h - hb.astype(F32)).astype(BF16)
    k_ref[...] = _bdot(hb, wk_ref[...]).astype(BF16)
    n = zt_ref.shape[0]
    for j in range(0, n - 1024, 1024):
        zt_ref[j:j + 1024, :] = lax.dot_general(wt_ref[j:j + 1024, :], hb, NT_DIMS,
                                                preferred_element_type=F32).astype(BF16)
    tail = lax.dot_general(wt_ref[n - 1024:, :], hb, NT_DIMS, preferred_element_type=F32)
    zt_ref[n - 1024:, :] = tail[0:1024].astype(BF16)
    corr = lax.dot_general(wt_ref[n:n + N_GATES, :], h_lo, NT_DIMS, preferred_element_type=F32)
    gr_ref[...] = tail[1024:1024 + N_GATES] + tail[1024 + N_GATES:] + corr + bgt_ref[...]


def _odd_in(x, g, wk, wt, bgt, tm):
    t, n = x.shape[0], wt.shape[0] - 2 * N_GATES
    return pl.pallas_call(
        _odd_in_kernel,
        grid=(t // tm,),
        in_specs=[pl.BlockSpec((tm, D), lambda i: (i, 0)), _full((1, D)), _full((D, D)),
                  _full((n + 2 * N_GATES, D)), _full((N_GATES, 1))],
        out_specs=[pl.BlockSpec((tm, D), lambda i: (i, 0)),
                   pl.BlockSpec((n, tm), lambda i: (0, i)),
                   pl.BlockSpec((N_GATES, tm), lambda i: (0, i))],
        out_shape=[jax.ShapeDtypeStruct((t, D), BF16),
                   jax.ShapeDtypeStruct((n, t), BF16),
                   jax.ShapeDtypeStruct((N_GATES, t), F32)],
        compiler_params=_params("parallel"),
        name="odd_in",
    )(x, g, wk, wt, bgt)


def _mlstm_kernel(qf_ref, kf_ref, vf_ref, qb_ref, kb_ref, vb_ref, grf_ref, grb_ref,
                  hf_ref, hb_ref, c_scr, n_scr, m_scr):
    @pl.when(pl.program_id(1) == 0)
    def _():
        c_scr[...] = jnp.zeros_like(c_scr)
        n_scr[...] = jnp.zeros_like(n_scr)
        m_scr[...] = jnp.zeros_like(m_scr)

    L = LSTM_CHUNK
    row = lax.broadcasted_iota(I32, (L, L), 0)
    col = lax.broadcasted_iota(I32, (L, L), 1)
    lower = row >= col
    upper = row <= col
    lower_b = jnp.where(lower, 1.0, 0.0).astype(BF16)
    upper_b = jnp.where(upper, 1.0, 0.0).astype(BF16)
    eye_b = jnp.where(row == col, 1.0, 0.0).astype(BF16)

    def split3(x):
        hi = x.astype(BF16)
        r = x - hi.astype(F32)
        mid = r.astype(BF16)
        return hi, mid, (r - mid.astype(F32)).astype(BF16)

    dirs = ((qf_ref, kf_ref, vf_ref, grf_ref, hf_ref),
            (qb_ref, kb_ref, vb_ref, grb_ref, hb_ref))
    for d, (q_ref, k_ref, v_ref, gr_ref, h_ref) in enumerate(dirs):
        grow = gr_ref[...]
        if d == 0:
            tri, mask_t, last = upper_b, upper, L - 1
        else:
            tri, mask_t, last = lower_b, lower, 0
        b_rows = sum(_bdot(p, tri) for p in split3(_log_sigmoid(grow)))
        src_rows = grow - pltpu.roll(b_rows, N_GATES - HEADS, 0)
        src_cols = sum(lax.dot_general(eye_b, p, NT_DIMS, preferred_element_type=F32) for p in split3(src_rows))
        i_off, f_off = 8 * d, 8 * d + HEADS
        k_scale = HEAD_DIM ** -0.5
        for hh in range(HEADS):
            idx = d * HEADS + hh
            cs = slice(hh * HEAD_DIM, (hh + 1) * HEAD_DIM)
            k = k_ref[:, cs]
            qt = q_ref[cs, :]
            vt = v_ref[cs, :]
            b_row = b_rows[f_off + hh:f_off + hh + 1, :]
            i_row = grow[i_off + hh:i_off + hh + 1, :]
            src_col = src_cols[:, i_off + hh:i_off + hh + 1]
            m = m_scr[idx][:, 0:1]
            cmat = c_scr[idx]
            nrows = n_scr[idx]

            dmat_t = jnp.where(mask_t, b_row + src_col, -jnp.inf)
            inter = b_row + m
            m_row = jnp.maximum(jnp.max(dmat_t, axis=0, keepdims=True), inter)
            smat_t = _bdot(k, qt) * (jnp.exp(dmat_t - m_row) * k_scale)
            w_inter = jnp.exp(inter - m_row)
            num_t = _bdot(vt, smat_t.astype(BF16)) + w_inter * _bdot(cmat.astype(BF16), qt)
            den = (jnp.sum(smat_t, axis=0, keepdims=True)
                   + w_inter * _bdot(nrows.astype(BF16), qt)[0:1, :])
            rden = 1.0 / jnp.maximum(jnp.abs(den), jnp.exp(-m_row))
            h_ref[cs, :] = (num_t * rden).astype(h_ref.dtype)

            b_last = b_row[:, last:last + 1]
            g_row = b_last - b_row + i_row
            m_new = jnp.maximum(b_last + m, jnp.max(g_row, axis=1, keepdims=True))
            wg = jnp.exp(g_row - m_new) * k_scale
            decay = jnp.exp(b_last + m - m_new)
            c_scr[idx] = decay * cmat + _bdot((vt.astype(F32) * wg).astype(BF16), k)
            n_scr[idx] = decay * nrows + _bdot(jnp.broadcast_to(wg, (8, L)).astype(BF16), k)
            m_scr[idx] = jnp.broadcast_to(m_new, (1, 128))


def _mlstm(k, zt, gr, batch, seq):
    t = k.shape[0]
    nc = seq // LSTM_CHUNK
    L = LSTM_CHUNK
    fwd = lambda b, c: (b * nc + c, 0)
    bwd = lambda b, c: (b * nc + nc - 1 - c, 0)

    def fwd_t(rb):
        return lambda b, c: (rb, b * nc + c)

    def bwd_t(rb):
        return lambda b, c: (rb, b * nc + nc - 1 - c)

    qkv = lambda f, ft: [pl.BlockSpec((D, L), ft(0)), pl.BlockSpec((L, D), f), pl.BlockSpec((D, L), ft(1))]
    return pl.pallas_call(
        _mlstm_kernel,
        grid=(batch, nc),
        in_specs=qkv(fwd, fwd_t) + qkv(bwd, bwd_t) + [
            pl.BlockSpec((N_GATES, L), fwd_t(0)), pl.BlockSpec((N_GATES, L), bwd_t(0)),
        ],
        out_specs=[pl.BlockSpec((D, L), fwd_t(0)), pl.BlockSpec((D, L), bwd_t(0))],
        out_shape=[jax.ShapeDtypeStruct((D, t), BF16), jax.ShapeDtypeStruct((D, t), BF16)],
        scratch_shapes=[pltpu.VMEM((2 * HEADS, HEAD_DIM, HEAD_DIM), F32),
                        pltpu.VMEM((2 * HEADS, 8, HEAD_DIM), F32),
                        pltpu.VMEM((2 * HEADS, 1, 128), F32)],
        compiler_params=_params("parallel", "arbitrary"),
        name="mlstm",
    )(zt, k, zt, zt, k, zt, gr, gr)


def _odd_out_kernel(hf_ref, hb_ref, og_ref, x_ref, hg_ref, wo_ref, o_ref, cat_scr):
    tm = x_ref.shape[0]
    for hh in range(HEADS):
        rs = slice(hh * HEAD_DIM, (hh + 1) * HEAD_DIM)
        s = hf_ref[rs, :].astype(F32) + hb_ref[rs, :].astype(F32)
        inv = lax.rsqrt(jnp.mean(s * s, axis=0, keepdims=True) + EPS)
        gate = _sigmoid(og_ref[rs, :].astype(F32))
        for c0 in range(0, tm, 128):
            cs = slice(c0, c0 + 128)
            cat_scr[rs, cs] = (gate[:, cs] * (s[:, cs] * inv[:, cs]) * hg_ref[rs, :]).astype(BF16)
    o_ref[...] = x_ref[...] + lax.dot_general(cat_scr[...], wo_ref[...], TN_DIMS, preferred_element_type=F32)


def _odd_out(hf, hb, zt, x, hg_rep, wo, tm):
    t = x.shape[0]
    col = lambda i: (0, i)
    return pl.pallas_call(
        _odd_out_kernel,
        grid=(t // tm,),
        in_specs=[pl.BlockSpec((D, tm), col), pl.BlockSpec((D, tm), col),
                  pl.BlockSpec((D, tm), lambda i: (2, i)), pl.BlockSpec((tm, D), lambda i: (i, 0)),
                  _full((D, 128)), _full((D, D))],
        out_specs=pl.BlockSpec((tm, D), lambda i: (i, 0)),
        out_shape=jax.ShapeDtypeStruct((t, D), F32),
        scratch_shapes=[pltpu.VMEM((D, tm), BF16)],
        compiler_params=_params("parallel"),
        name="odd_out",
    )(hf, hb, zt, x, hg_rep, wo)


def _xattn_kernel(x_ref, g_ref, kv_ref, wq_ref, wo_ref, gr_ref, wr_ref, o_ref, aff_ref, cat_scr):
    x = x_ref[...]
    h = _rms(x, g_ref[...]).astype(BF16)
    q = (_bdot(h, wq_ref[...]) * (HEAD_DIM ** -0.5)).astype(BF16)
    for hh in range(HEADS):
        cs = slice(hh * HEAD_DIM, (hh + 1) * HEAD_DIM)
        k = kv_ref[:, cs]
        v = kv_ref[:, D + hh * HEAD_DIM:D + (hh + 1) * HEAD_DIM]
        s = lax.dot_general(q[:, cs], k, NT_DIMS, preferred_element_type=F32)
        p = jnp.exp(s - jnp.max(s, axis=-1, keepdims=True))
        p = p / jnp.sum(p, axis=-1, keepdims=True)
        cat_scr[:, cs] = _bdot(p.astype(BF16), v).astype(BF16)
    y = x + _bdot(cat_scr[...], wo_ref[...])
    o_ref[...] = y

    hr = _rms(y, gr_ref[...])
    hr_hi = hr.astype(BF16)
    hr_lo = (hr - hr_hi.astype(F32)).astype(BF16)
    main = lax.dot_general(wr_ref[...], hr_hi, NT_DIMS, preferred_element_type=F32)
    corr = lax.dot_general(wr_ref[0:N_EXPERTS, :], hr_lo, NT_DIMS, preferred_element_type=F32)
    logits = main[0:N_EXPERTS] + main[N_EXPERTS:] + corr
    p = jnp.exp(logits - jnp.max(logits, axis=0, keepdims=True))
    aff_ref[...] = p / jnp.sum(p, axis=0, keepdims=True)


def _xattn(x, g, kv, wq, wo, g_ffn, wr, seq):
    t = x.shape[0]
    ts = min(ROW_TILE, seq)
    tiles_per_seq = seq // ts
    return pl.pallas_call(
        _xattn_kernel,
        grid=(t // ts,),
        in_specs=[pl.BlockSpec((ts, D), lambda i: (i, 0)), _full((1, D)),
                  pl.BlockSpec((MEM_LEN, 2 * D), lambda i: (i // tiles_per_seq, 0)),
                  _full((D, D)), _full((D, D)), _full((1, D)), _full((2 * N_EXPERTS, D))],
        out_specs=[pl.BlockSpec((ts, D), lambda i: (i, 0)), pl.BlockSpec((N_EXPERTS, ts), lambda i: (0, i))],
        out_shape=[jax.ShapeDtypeStruct((t, D), F32), jax.ShapeDtypeStruct((N_EXPERTS, t), F32)],
        scratch_shapes=[pltpu.VMEM((ts, D), BF16)],
        compiler_params=_params("parallel"),
        name="xattn",
    )(x, g, kv, wq, wo, g_ffn, wr)


def _select_kernel(aff_ref, rank_ref, p0_ref, n_ref, *, cap, nblk):
    a = aff_ref[0]
    bits = pltpu.bitcast(a, I32)
    capf = float(cap)

    def count(m):
        return jnp.sum(jnp.where(m, 1.0, 0.0), keepdims=True)

    def bisect(_, carry):
        lo, hi = carry
        mid = lo + ((hi - lo + 1) >> 1)
        ok = count(bits >= mid) >= capf
        return jnp.where(ok, mid, lo), jnp.where(ok, hi, mid - 1)

    lo, _ = lax.fori_loop(0, 31, bisect,
                          (jnp.zeros((1, 1), I32), jnp.full((1, 1), 0x7F800000, I32)))
    gt = bits > lo
    eq = bits == lo
    need = capf - count(gt)

    r = lax.broadcasted_iota(I32, (MOE_BLOCK, MOE_BLOCK), 0)
    c = lax.broadcasted_iota(I32, (MOE_BLOCK, MOE_BLOCK), 1)
    incl_mat = jnp.where(r <= c, 1.0, 0.0).astype(BF16)
    rb = lax.broadcasted_iota(I32, (nblk, nblk), 0)
    cb = lax.broadcasted_iota(I32, (nblk, nblk), 1)
    before = jnp.where(rb > cb, 1.0, 0.0)

    def block_offsets(tot):
        return jnp.dot(before, jnp.broadcast_to(tot, (nblk, 128)), precision=HIGHEST,
                       preferred_element_type=F32)

    incl_eq = _bdot(jnp.where(eq, 1.0, 0.0).astype(BF16), incl_mat)
    rank_eq = block_offsets(incl_eq[:, MOE_BLOCK - 1:MOE_BLOCK])[:, 0:1] + incl_eq - 1.0
    sel = jnp.logical_or(gt, jnp.logical_and(eq, rank_eq < need))

    incl = _bdot(jnp.where(sel, 1.0, 0.0).astype(BF16), incl_mat)
    n = incl[:, MOE_BLOCK - 1:MOE_BLOCK]
    n_aligned = jnp.floor((n + (SLOT_ALIGN - 1)) / SLOT_ALIGN) * SLOT_ALIGN
    rank_ref[0] = jnp.where(sel, incl - 1.0, -1.0).astype(I32)
    p0_ref[0] = block_offsets(n_aligned).astype(I32)
    n_ref[0] = jnp.broadcast_to(n, (nblk, 128)).astype(I32)


def _select(aff3, cap):
    _, nblk, _ = aff3.shape
    blk = lambda w: pl.BlockSpec((1, nblk, w), lambda e: (e, 0, 0))
    return pl.pallas_call(
        functools.partial(_select_kernel, cap=cap, nblk=nblk),
        grid=(N_EXPERTS,),
        in_specs=[blk(MOE_BLOCK)],
        out_specs=[blk(MOE_BLOCK), blk(128), blk(128)],
        out_shape=[jax.ShapeDtypeStruct((N_EXPERTS, nblk, MOE_BLOCK), I32),
                   jax.ShapeDtypeStruct((N_EXPERTS, nblk, 128), I32),
                   jax.ShapeDtypeStruct((N_EXPERTS, nblk, 128), I32)],
        compiler_params=_params("parallel"),
        name="select",
    )(aff3)


def _pack_kernel(p0_ref, n_ref, pend_ref, x_ref, g_ref, rank_ref, xe_ref, h_scr, p_scr, stage, extra,
                 sem, esem, *, nblk, cap, cap_x):
    b = pl.program_id(0)
    slot = b % 2

    def copy(e, start, slot):
        return pltpu.make_async_copy(stage.at[slot, pl.ds(e * SLOT_WINDOW, SLOT_WINDOW), :],
                                     xe_ref.at[e, pl.ds(start, SLOT_WINDOW), :], sem.at[e])

    def extra_copy(e, start):
        return pltpu.make_async_copy(extra, xe_ref.at[e, pl.ds(start, SLOT_WINDOW), :], esem.at[0])

    def window_start(e, w):
        return pl.multiple_of(p0_ref[e, b] + SLOT_WINDOW * w, SLOT_ALIGN)

    h_scr[...] = _rms(x_ref[...], g_ref[...]).astype(BF16)
    slot_id = lax.broadcasted_iota(I32, (SLOT_WINDOW, MOE_BLOCK), 0)
    for e in range(N_EXPERTS):
        p_scr[e * SLOT_WINDOW:(e + 1) * SLOT_WINDOW, :] = jnp.where(
            rank_ref[e, 0] == slot_id, 1.0, 0.0).astype(BF16)
    for r0 in range(0, N_EXPERTS * SLOT_WINDOW, MOE_BLOCK):
        stage[slot, r0:r0 + MOE_BLOCK, :] = _bdot(p_scr[r0:r0 + MOE_BLOCK, :], h_scr[...]).astype(BF16)

    for e in range(N_EXPERTS):
        @pl.when(b > 0)
        def _(e=e):
            copy(e, 0, 1 - slot).wait()
        copy(e, window_start(e, 0), slot).start()
        for w in range(1, MAX_WINDOWS):
            @pl.when(n_ref[e, b] > SLOT_WINDOW * w)
            def _(e=e, w=w):
                onehot = jnp.where(rank_ref[e, 0] - SLOT_WINDOW * w == slot_id, 1.0, 0.0).astype(BF16)
                extra[...] = _bdot(onehot, h_scr[...]).astype(BF16)
                c = extra_copy(e, window_start(e, w))
                c.start()
                c.wait()

    @pl.when(b == nblk - 1)
    def _():
        for e in range(N_EXPERTS):
            copy(e, 0, slot).wait()
        extra[...] = jnp.zeros_like(extra)
        n_zero = -(-(cap_x - cap) // SLOT_WINDOW)
        last_start = cap_x - SLOT_WINDOW
        for phase in ("start", "wait"):
            for e in range(N_EXPERTS):
                for z in range(n_zero):
                    start = pend_ref[e] + SLOT_WINDOW * z

                    @pl.when(start < last_start)
                    def _(e=e, start=start, phase=phase):
                        c = extra_copy(e, pl.multiple_of(start, SLOT_ALIGN))
                        c.start() if phase == "start" else c.wait()
        for e in range(N_EXPERTS):
            extra_copy(e, last_start).start()
        for e in range(N_EXPERTS):
            extra_copy(e, last_start).wait()


def _pack(p0, n, p_end, x, g, rank4, cap_x):
    t = x.shape[0]
    nblk = t // MOE_BLOCK
    grid_spec = pltpu.PrefetchScalarGridSpec(
        num_scalar_prefetch=3,
        grid=(nblk,),
        in_specs=[pl.BlockSpec((MOE_BLOCK, D), lambda b, *_: (b, 0)),
                  pl.BlockSpec((1, D), lambda b, *_: (0, 0)),
                  pl.BlockSpec((N_EXPERTS, 1, 1, MOE_BLOCK), lambda b, *_: (0, b, 0, 0))],
        out_specs=pl.BlockSpec(memory_space=pl.ANY),
        scratch_shapes=[pltpu.VMEM((MOE_BLOCK, D), BF16),
                        pltpu.VMEM((N_EXPERTS * SLOT_WINDOW, MOE_BLOCK), BF16),
                        pltpu.VMEM((2, N_EXPERTS * SLOT_WINDOW, D), BF16),
                        pltpu.VMEM((SLOT_WINDOW, D), BF16),
                        pltpu.SemaphoreType.DMA((N_EXPERTS,)),
                        pltpu.SemaphoreType.DMA((1,))],
    )
    return pl.pallas_call(
        functools.partial(_pack_kernel, nblk=nblk, cap=CAPACITY_FACTOR * t // N_EXPERTS, cap_x=cap_x),
        grid_spec=grid_spec,
        out_shape=jax.ShapeDtypeStruct((N_EXPERTS, cap_x, D), BF16),
        compiler_params=_params("arbitrary"),
        name="pack",
    )(p0, n, p_end, x, g, rank4)


def _ffn_kernel(pend_ref, xe_ref, w1_ref, w3_ref, w2_ref, ye_ref, *, tm):
    e, i = pl.program_id(0), pl.program_id(1)
    active = i * tm < pend_ref[e]

    @pl.when(active)
    def _():
        xb = xe_ref[0]
        acc = jnp.zeros((tm, D), F32)
        for f in range(0, EXPERT_FF, FF_CHUNK):
            h1 = _bdot(xb, w1_ref[0, :, f:f + FF_CHUNK])
            h3 = _bdot(xb, w3_ref[0, :, f:f + FF_CHUNK])
            hid = (h1 * _sigmoid(h1) * h3).astype(BF16)
            acc = acc + _bdot(hid, w2_ref[0, f:f + FF_CHUNK, :])
        ye_ref[0] = acc.astype(BF16)

    @pl.when(jnp.logical_not(active))
    def _():
        ye_ref[0] = jnp.zeros((tm, D), BF16)


def _ffn(p_end, xe, w1, w3, w2):
    _, cap_x, _ = xe.shape
    tm = FFN_TILE
    grid_spec = pltpu.PrefetchScalarGridSpec(
        num_scalar_prefetch=1,
        grid=(N_EXPERTS, cap_x // tm),
        in_specs=[pl.BlockSpec((1, tm, D), lambda e, i, *_: (e, i, 0)),
                  pl.BlockSpec((1, D, EXPERT_FF), lambda e, i, *_: (e, 0, 0)),
                  pl.BlockSpec((1, D, EXPERT_FF), lambda e, i, *_: (e, 0, 0)),
                  pl.BlockSpec((1, EXPERT_FF, D), lambda e, i, *_: (e, 0, 0))],
        out_specs=pl.BlockSpec((1, tm, D), lambda e, i, *_: (e, i, 0)),
    )
    return pl.pallas_call(
        functools.partial(_ffn_kernel, tm=tm),
        grid_spec=grid_spec,
        out_shape=jax.ShapeDtypeStruct(xe.shape, BF16),
        compiler_params=_params("parallel", "arbitrary"),
        name="ffn",
    )(p_end, xe, w1, w3, w2)


def _unpack_kernel(p0_ref, n_ref, x_ref, rank_ref, gate_ref, gfin_ref, ye_ref, o_ref, slab, slab_now, u_scr,
                   extra, sem, esem, *, nblk, final_norm):
    b = pl.program_id(0)
    slot = b % 2
    rows = N_EXPERTS * SLOT_WINDOW

    def copy(e, blk, slot):
        start = pl.multiple_of(p0_ref[e, blk], SLOT_ALIGN)
        return pltpu.make_async_copy(ye_ref.at[e, pl.ds(start, SLOT_WINDOW), :],
                                     slab.at[slot, pl.ds(e * SLOT_WINDOW, SLOT_WINDOW), :], sem.at[slot])

    @pl.when(b == 0)
    def _():
        for e in range(N_EXPERTS):
            copy(e, 0, 0).start()

    @pl.when(b + 1 < nblk)
    def _():
        for e in range(N_EXPERTS):
            copy(e, b + 1, 1 - slot).start()

    lane = lax.broadcasted_iota(I32, (MOE_BLOCK, SLOT_WINDOW), 1)
    for e in range(N_EXPERTS):
        wb = jnp.broadcast_to(rank_ref[:, e:e + 1], (MOE_BLOCK, SLOT_WINDOW))
        u_scr[:, e * SLOT_WINDOW:(e + 1) * SLOT_WINDOW] = jnp.where(
            (wb >> 16) == lane, pltpu.bitcast(wb << 16, F32), 0.0).astype(BF16)
    for e in range(N_EXPERTS):
        copy(e, b, slot).wait()
    slab_now[...] = slab[slot]
    o_ref[...] = x_ref[...] + _bdot(u_scr[...], slab_now[...])

    for e in range(N_EXPERTS):
        for w in range(1, MAX_WINDOWS):
            @pl.when(n_ref[e, b] > SLOT_WINDOW * w)
            def _(e=e, w=w):
                start = pl.multiple_of(p0_ref[e, b] + SLOT_WINDOW * w, SLOT_ALIGN)
                c = pltpu.make_async_copy(ye_ref.at[e, pl.ds(start, SLOT_WINDOW), :], extra, esem.at[0])
                c.start()
                c.wait()
                onehot = jnp.where((rank_ref[:, e:e + 1] >> 16) - SLOT_WINDOW * w == lane, 1.0, 0.0).astype(BF16)
                o_ref[...] += gate_ref[:, e:e + 1] * _bdot(onehot, extra[...])

    if final_norm:
        o_ref[...] = _rms(o_ref[...], gfin_ref[...])


def _unpack(p0, n, x, rank_t, gate_t, gfin, ye, final_norm):
    t = x.shape[0]
    nblk = t // MOE_BLOCK
    rows = N_EXPERTS * SLOT_WINDOW
    grid_spec = pltpu.PrefetchScalarGridSpec(
        num_scalar_prefetch=2,
        grid=(nblk,),
        in_specs=[pl.BlockSpec((MOE_BLOCK, D), lambda b, *_: (b, 0)),
                  pl.BlockSpec((MOE_BLOCK, N_EXPERTS), lambda b, *_: (b, 0)),
                  pl.BlockSpec((MOE_BLOCK, N_EXPERTS), lambda b, *_: (b, 0)),
                  pl.BlockSpec((1, D), lambda b, *_: (0, 0)),
                  pl.BlockSpec(memory_space=pl.ANY)],
        out_specs=pl.BlockSpec((MOE_BLOCK, D), lambda b, *_: (b, 0)),
        scratch_shapes=[pltpu.VMEM((2, rows, D), BF16),
                        pltpu.VMEM((rows, D), BF16),
                        pltpu.VMEM((MOE_BLOCK, rows), BF16),
                        pltpu.VMEM((SLOT_WINDOW, D), BF16),
                        pltpu.SemaphoreType.DMA((2,)),
                        pltpu.SemaphoreType.DMA((1,))],
    )
    return pl.pallas_call(
        functools.partial(_unpack_kernel, nblk=nblk, final_norm=final_norm),
        grid_spec=grid_spec,
        out_shape=jax.ShapeDtypeStruct((t, D), F32),
        compiler_params=_params("arbitrary"),
        name="unpack",
    )(p0, n, x, rank_t, gate_t, gfin, ye)


def _moe(x, aff_t, g, w1, w3, w2, gfin, final_norm):
    t = x.shape[0]
    nblk = t // MOE_BLOCK
    cap = CAPACITY_FACTOR * t // N_EXPERTS
    cap_x = -(-(cap + SLOT_ALIGN * nblk + FFN_TILE) // FFN_TILE) * FFN_TILE

    rank, p0w, nw = _select(aff_t.reshape(N_EXPERTS, nblk, MOE_BLOCK), cap)
    p0, n = p0w[:, :, 0], nw[:, :, 0]
    p_end = p0[:, -1] + (n[:, -1] + SLOT_ALIGN - 1) // SLOT_ALIGN * SLOT_ALIGN

    xe = _pack(p0, n, p_end, x, g, rank.reshape(N_EXPERTS, nblk, 1, MOE_BLOCK), cap_x)
    ye = _ffn(p_end, xe, w1, w3, w2)
    gate_bits = lax.bitcast_convert_type(aff_t.T.astype(BF16), jnp.uint16).astype(I32)
    word_t = (rank.reshape(N_EXPERTS, t).T << 16) | gate_bits
    return _unpack(p0, n, x, word_t, aff_t.T, gfin, ye, final_norm)


def _trunk(x3, mem3, w):
    batch, seq, _ = x3.shape
    t = batch * seq
    tm = min(ROW_TILE, seq)
    x = x3.reshape(t, D)
    mem = mem3.reshape(batch * MEM_LEN, D)
    row = lambda v: v.reshape(1, -1)
    depth = w["norm_mix"].shape[0]
    for l in range(depth):
        j = l // 2
        g_mix = row(w["norm_mix"][l])
        if l % 2 == 0:
            z = _proj(x, g_mix, w["even_w_in"][j], norm=True, tm=tm)
            x = _even_out(z, x, seq, w["a_ws"][j], w["a_bs"][j].T, row(w["a_ln_g"][j]), row(w["a_ln_b"][j]),
                          w["b_conv_w"][j], row(w["b_conv_b"][j]), row(w["b_ln_g"][j]), row(w["b_ln_b"][j]),
                          w["even_w_out"][j])
        else:
            k, zt, gr = _odd_in(x, g_mix, w["odd_w_k"][j], w["odd_w_t"][j], w["odd_b_gate"][j].reshape(-1, 1), tm)
            hf, hb = _mlstm(k, zt, gr, batch, seq)
            hg_rep = jnp.broadcast_to(w["odd_hnorm_g"][j].reshape(D, 1), (D, 128))
            x = _odd_out(hf, hb, zt, x, hg_rep, w["odd_w_out"][j], tm)
        kv = _proj(mem, g_mix, w["xa_wkv"][l], norm=False, tm=MEM_LEN)
        g_ffn = row(w["norm_ffn"][l])
        x, aff_t = _xattn(x, row(w["norm_xattn"][l]), kv, w["xa_wq"][l], w["xa_wo"][l], g_ffn,
                          w["moe_router_t"][l], seq)
        x = _moe(x, aff_t, g_ffn, w["moe_w1"][l], w["moe_w3"][l], w["moe_w2"][l], row(w["norm_final"]),
                 final_norm=(l == depth - 1))
    return x.reshape(batch, seq, D)


def _prepare_weights(norm_mix, norm_xattn, norm_ffn, even_w_in, a_ws, a_bs, a_ln_g, a_ln_b, b_conv_w, b_conv_b,
                     b_ln_g, b_ln_b, even_w_out, odd_w_in, odd_b_gate, odd_hnorm_g, odd_w_out, xa_wq, xa_wkv,
                     xa_wo, moe_router, moe_w1, moe_w3, moe_w2, norm_final):
    bf = lambda v: v.astype(BF16)
    router_t = jnp.swapaxes(moe_router, 1, 2)
    router_hi = bf(router_t)
    router_lo = bf(router_t - router_hi.astype(F32))
    gate_t = jnp.swapaxes(odd_w_in[:, :, 4 * D:], 1, 2)
    gate_hi = bf(gate_t)
    gate_lo = bf(gate_t - gate_hi.astype(F32))
    w = dict(
        norm_mix=norm_mix, norm_xattn=norm_xattn, norm_ffn=norm_ffn, norm_final=norm_final,
        even_w_in=bf(even_w_in), a_ws=bf(a_ws), a_bs=a_bs, a_ln_g=a_ln_g, a_ln_b=a_ln_b,
        b_conv_w=b_conv_w, b_conv_b=b_conv_b, b_ln_g=b_ln_g, b_ln_b=b_ln_b, even_w_out=bf(even_w_out),
        odd_w_t=jnp.concatenate([bf(jnp.swapaxes(odd_w_in[:, :, 0:D], 1, 2)),
                                 bf(jnp.swapaxes(odd_w_in[:, :, 2 * D:4 * D], 1, 2)), gate_hi, gate_lo], axis=1),
        odd_w_k=bf(odd_w_in[:, :, D:2 * D]), odd_b_gate=odd_b_gate,
        odd_hnorm_g=odd_hnorm_g, odd_w_out=bf(odd_w_out),
        xa_wq=bf(xa_wq), xa_wkv=bf(xa_wkv), xa_wo=bf(xa_wo),
        moe_router_t=jnp.concatenate([router_hi, router_lo], axis=1),
        moe_w1=bf(moe_w1), moe_w3=bf(moe_w3), moe_w2=bf(moe_w2),
    )
    return w


def kernel(x_prompt, x_sample, mem_prompt, mem_sample, norm_mix, norm_xattn, norm_ffn, even_w_in, a_ws, a_bs, a_ln_g, a_ln_b, b_conv_w, b_conv_b, b_ln_g, b_ln_b, even_w_out, odd_w_in, odd_b_gate, odd_hnorm_g, odd_w_out, xa_wq, xa_wkv, xa_wo, moe_router, moe_w1, moe_w3, moe_w2, norm_final):
    w = _prepare_weights(norm_mix, norm_xattn, norm_ffn, even_w_in, a_ws, a_bs, a_ln_g, a_ln_b, b_conv_w,
                         b_conv_b, b_ln_g, b_ln_b, even_w_out, odd_w_in, odd_b_gate, odd_hnorm_g, odd_w_out,
                         xa_wq, xa_wkv, xa_wo, moe_router, moe_w1, moe_w3, moe_w2, norm_final)
    return _trunk(x_prompt, mem_prompt, w), _trunk(x_sample, mem_sample, w)
```
